```python
import math
import jax, jax.numpy as jnp
from jax import lax
import numpy as np

D_MODEL = 1024
BATCH = 4
SEQ = 8192
DEPTH = 2

HEAD_DIM = 64
N_FOX_HEADS = 8
N_MOBA_HEADS = 8
FOX_WIDTH = N_FOX_HEADS * HEAD_DIM
MOBA_WIDTH = N_MOBA_HEADS * HEAD_DIM
MIX_WIDTH = FOX_WIDTH + MOBA_WIDTH
IN_COLS = 3 * FOX_WIDTH + N_FOX_HEADS + 3 * MOBA_WIDTH
FOX_Q_BLOCK = 128
MOBA_BLOCK = 256
MOBA_TOPK = 3
MOBA_Q_CHUNK = 64
D_FF = -(-8 * D_MODEL // (3 * 256)) * 256
ALPHA = (2 * DEPTH) ** 0.25
BETA = (8 * DEPTH) ** -0.25
LN_EPS = 1e-5
NEG_INF = -1e30

kernel_name = "fox_moba_hybrid_deepnorm"


def layer_norm(x, g, b):
    xf = x.astype(jnp.float32)
    mu = xf.mean(-1, keepdims=True)
    var = jnp.square(xf - mu).mean(-1, keepdims=True)
    y = (xf - mu) * lax.rsqrt(var + LN_EPS)
    return (y * g + b).astype(x.dtype)


def split_heads(t, n_heads):
    B, S, _ = t.shape
    return t.reshape(B, S, n_heads, HEAD_DIM).transpose(0, 2, 1, 3)


def alibi_slopes(n_heads):
    return jnp.asarray([2.0 ** (-8.0 * (i + 1) / n_heads) for i in range(n_heads)], dtype=jnp.float32)


def fox_attention(q, k, v, log_f):
    B, H, S, _ = q.shape
    scale = HEAD_DIM ** -0.5
    c = lax.cumsum(log_f, axis=2)
    n_blocks = S // FOX_Q_BLOCK
    q_blk = q.reshape(B, H, n_blocks, FOX_Q_BLOCK, HEAD_DIM).transpose(2, 0, 1, 3, 4)
    c_blk = c.reshape(B, H, n_blocks, FOX_Q_BLOCK).transpose(2, 0, 1, 3)
    k_pos = jnp.arange(S)

    def one_block(args):
        i, qb, cb = args
        q_pos = i * FOX_Q_BLOCK + jnp.arange(FOX_Q_BLOCK)
        s = jnp.einsum('bhqd,bhkd->bhqk', qb, k).astype(jnp.float32) * scale
        s = s + cb[..., :, None] - c[..., None, :]
        s = jnp.where(k_pos[None, :] <= q_pos[:, None], s, NEG_INF)
        p = jax.nn.softmax(s, axis=-1)
        return jnp.einsum('bhqk,bhkd->bhqd', p.astype(v.dtype), v)

    out = lax.map(one_block, (jnp.arange(n_blocks), q_blk, c_blk))
    return out.transpose(1, 2, 0, 3, 4).reshape(B, H, S, HEAD_DIM)


def moba_attention(q, k, v, slopes):
    B, H, S, _ = q.shape
    scale = HEAD_DIM ** -0.5
    n_kb = -(-S // MOBA_BLOCK)
    pad = n_kb * MOBA_BLOCK - S
    k_p = jnp.pad(k, ((0, 0), (0, 0), (0, pad), (0, 0)))
    v_p = jnp.pad(v, ((0, 0), (0, 0), (0, pad), (0, 0)))
    k_blocks = k_p.reshape(B, H, n_kb, MOBA_BLOCK, HEAD_DIM)
    v_blocks = v_p.reshape(B, H, n_kb, MOBA_BLOCK, HEAD_DIM)
    k_mean = k_blocks.astype(jnp.float32).mean(axis=3).astype(k.dtype)
    top_k = min(MOBA_TOPK, n_kb)
    n_chunks = S // MOBA_Q_CHUNK
    q_chunks = q.reshape(B, H, n_chunks, MOBA_Q_CHUNK, HEAD_DIM).transpose(2, 0, 1, 3, 4)
    blk_ids = jnp.arange(n_kb)
    in_blk = jnp.arange(MOBA_BLOCK)
    b_idx = jnp.arange(B)[:, None, None, None]
    h_idx = jnp.arange(H)[None, :, None, None]

    def one_chunk(args):
        i, qc = args
        q_pos = i * MOBA_Q_CHUNK + jnp.arange(MOBA_Q_CHUNK)
        own = (i * MOBA_Q_CHUNK) // MOBA_BLOCK
        g = jnp.einsum('bhqd,bhnd->bhqn', qc, k_mean).astype(jnp.float32)
        g = jnp.where(blk_ids < own, g, NEG_INF)
        _, sel = lax.top_k(g, top_k)
        valid = sel < own
        k_sel = k_blocks[b_idx, h_idx, sel]
        v_sel = v_blocks[b_idx, h_idx, sel]
        s_sel = jnp.einsum('bhqd,bhqnkd->bhqnk', qc, k_sel).astype(jnp.float32) * scale
        pos_sel = sel[..., None] * MOBA_BLOCK + in_blk
        dist_sel = (q_pos[None, None, :, None, None] - pos_sel).astype(jnp.float32)
        s_sel = s_sel - slopes[None, :, None, None, None] * dist_sel
        s_sel = jnp.where(valid[..., None], s_sel, NEG_INF)
        k_own = lax.dynamic_index_in_dim(k_blocks, own, axis=2, keepdims=False)
        v_own = lax.dynamic_index_in_dim(v_blocks, own, axis=2, keepdims=False)
        pos_own = own * MOBA_BLOCK + in_blk
        s_own = jnp.einsum('bhqd,bhkd->bhqk', qc, k_own).astype(jnp.float32) * scale
        dist_own = (q_pos[:, None] - pos_own[None, :]).astype(jnp.float32)
        s_own = s_own - slopes[None, :, None, None] * dist_own
        s_own = jnp.where(pos_own[None, :] <= q_pos[:, None], s_own, NEG_INF)
        n_sel = top_k * MOBA_BLOCK
        s_all = jnp.concatenate([s_sel.reshape(B, H, MOBA_Q_CHUNK, n_sel), s_own], axis=-1)
        p = jax.nn.softmax(s_all, axis=-1).astype(v.dtype)
        p_sel = p[..., :n_sel].reshape(B, H, MOBA_Q_CHUNK, top_k, MOBA_BLOCK)
        p_own = p[..., n_sel:]
        return (jnp.einsum('bhqnk,bhqnkd->bhqd', p_sel, v_sel)
                + jnp.einsum('bhqk,bhkd->bhqd', p_own, v_own))

    out = lax.map(one_chunk, (jnp.arange(n_chunks), q_chunks))
    return out.transpose(1, 2, 0, 3, 4).reshape(B, H, S, HEAD_DIM)


def hybrid_mixer(h, w_in, b_f, w_o):
    B, S, _ = h.shape
    proj = jnp.einsum('bsd,de->bse', h, w_in)
    cuts = [FOX_WIDTH, 2 * FOX_WIDTH, 3 * FOX_WIDTH, 3 * FOX_WIDTH + N_FOX_HEADS,
            3 * FOX_WIDTH + N_FOX_HEADS + MOBA_WIDTH, 3 * FOX_WIDTH + N_FOX_HEADS + 2 * MOBA_WIDTH]
    fq, fk, fv, f_logit, mq, mk, mv = jnp.split(proj, cuts, axis=-1)
    log_f = jax.nn.log_sigmoid(f_logit.astype(jnp.float32) + b_f.astype(jnp.float32))
    log_f = log_f.transpose(0, 2, 1)
    out_fox = fox_attention(split_heads(fq, N_FOX_HEADS), split_heads(fk, N_FOX_HEADS),
                            split_heads(fv, N_FOX_HEADS), log_f)
    out_moba = moba_attention(split_heads(mq, N_MOBA_HEADS), split_heads(mk, N_MOBA_HEADS),
                              split_heads(mv, N_MOBA_HEADS), alibi_slopes(N_MOBA_HEADS))
    heads = jnp.concatenate([out_fox, out_moba], axis=1)
    heads = heads.transpose(0, 2, 1, 3).reshape(B, S, MIX_WIDTH)
    return jnp.einsum('bse,ed->bsd', heads, w_o)


def swiglu_ffn(h, w_gu, w_down):
    gu = jnp.einsum('bsd,df->bsf', h, w_gu)
    gate, up = jnp.split(gu, [D_FF], axis=-1)
    return jnp.einsum('bsf,fd->bsd', jax.nn.silu(gate) * up, w_down)


def setup_inputs(seed: int = 0) -> dict:
    key = jax.random.key(seed)
    ks = jax.random.split(key, 10)
    x = jax.random.normal(ks[0], (BATCH, SEQ, D_MODEL), jnp.float32)
    w_in = jax.random.normal(ks[1], (DEPTH, D_MODEL, IN_COLS), jnp.float32) * D_MODEL ** -0.5
    col_scale = np.ones((IN_COLS,), np.float32)
    col_scale[2 * FOX_WIDTH:3 * FOX_WIDTH] = BETA
    col_scale[3 * FOX_WIDTH + N_FOX_HEADS + 2 * MOBA_WIDTH:] = BETA
    w_in = w_in * jnp.asarray(col_scale)
    b_f = jnp.linspace(1.0, 5.0, N_FOX_HEADS, dtype=jnp.float32)[None, :] \
        + 0.1 * jax.random.normal(ks[2], (DEPTH, N_FOX_HEADS), jnp.float32)
    w_o = jax.random.normal(ks[3], (DEPTH, MIX_WIDTH, D_MODEL), jnp.float32) * (MIX_WIDTH ** -0.5 * BETA)
    ln1_g = 1.0 + 0.05 * jax.random.normal(ks[4], (DEPTH, D_MODEL), jnp.float32)
    ln1_b = 0.02 * jax.random.normal(ks[5], (DEPTH, D_MODEL), jnp.float32)
    w_gu = jax.random.normal(ks[6], (DEPTH, D_MODEL, 2 * D_FF), jnp.float32) * D_MODEL ** -0.5
    w_down = jax.random.normal(ks[7], (DEPTH, D_FF, D_MODEL), jnp.float32) * (D_FF ** -0.5 * BETA)
    ln2_g = 1.0 + 0.05 * jax.random.normal(ks[8], (DEPTH, D_MODEL), jnp.float32)
    ln2_b = 0.02 * jax.random.normal(ks[9], (DEPTH, D_MODEL), jnp.float32)
    return {"x": x, "w_in": w_in, "b_f": b_f, "w_o": w_o, "ln1_g": ln1_g, "ln1_b": ln1_b,
            "w_gu": w_gu, "w_down": w_down, "ln2_g": ln2_g, "ln2_b": ln2_b}


def reference(x, w_in, b_f, w_o, ln1_g, ln1_b, w_gu, w_down, ln2_g, ln2_b):
    for l in range(DEPTH):
        x = layer_norm(ALPHA * x + hybrid_mixer(x, w_in[l], b_f[l], w_o[l]), ln1_g[l], ln1_b[l])
        x = layer_norm(ALPHA * x + swiglu_ffn(x, w_gu[l], w_down[l]), ln2_g[l], ln2_b[l])
    return x
```

```python
import functools

import jax
import jax.numpy as jnp
import numpy as np
from jax import lax
from jax.experimental import pallas as pl
from jax.experimental.pallas import tpu as pltpu

D_MODEL = 1024
HEAD_DIM = 64
N_FOX = 8
N_MOBA = 8
N_HEADS = N_FOX + N_MOBA
GROUP_WIDTH = N_FOX * HEAD_DIM
MOBA_BLOCK = 256
MOBA_TOPK = 3
D_FF = 2816
DEPTH = 2
ALPHA = (2 * DEPTH) ** 0.25
LN_EPS = 1e-5
NEG_INF = -1e30

TILE = 256
K_WIDTH = 128
AUG = K_WIDTH - HEAD_DIM
V_ROWS = 80
LF_ROWS = 16
VMEM_LIMIT = 56 * 1024 * 1024

F32 = jnp.float32
BF16 = jnp.bfloat16

FOX_Q_C = 0
FOX_K_ONE = 0
MOBA_SEL = 0
MOBA_ALIBI = 32


def _dot(a, b):
    return jnp.dot(a, b, preferred_element_type=F32)


def _dot_nt(a, b):
    return lax.dot_general(a, b, (((1,), (1,)), ((), ())), preferred_element_type=F32)


def _split3(x):
    hi = x.astype(BF16).astype(F32)
    r = x - hi
    mid = r.astype(BF16).astype(F32)
    lo = (r - mid).astype(BF16).astype(F32)
    return hi, mid, lo


def _log_sigmoid(x):
    return jnp.minimum(x, 0.0) - jnp.log1p(jnp.exp(-jnp.abs(x)))


def _alibi_slope(h):
    return 2.0 ** (-8.0 * (h + 1) / N_MOBA)


def _proj_kernel(x_ref, wn_ref, wt_ref, bf_ref, place_ref, k_ref, qt_ref, vt_ref,
                 carry_ref, kmean_ref):
    i = pl.program_id(1)

    @pl.when(i == 0)
    def _():
        carry_ref[...] = jnp.zeros_like(carry_ref)
        kmean_ref[...] = jnp.zeros_like(kmean_ref)

    xb = x_ref[0].astype(BF16)
    nat = _dot(xb, wn_ref[...])
    tr = _dot_nt(wt_ref[...], xb)

    row64 = lax.broadcasted_iota(jnp.int32, (AUG, TILE), 0)
    row_t = lax.broadcasted_iota(jnp.int32, (TILE, K_WIDTH), 0)
    lane_t = lax.broadcasted_iota(jnp.int32, (TILE, K_WIDTH), 1)
    ones_rows = jnp.where(lax.broadcasted_iota(jnp.int32, (V_ROWS - HEAD_DIM, TILE), 0) == 0,
                          1.0, 0.0).astype(BF16)

    lf = _log_sigmoid(tr[4 * GROUP_WIDTH:4 * GROUP_WIDTH + LF_ROWS] + bf_ref[...])
    tri = (lax.broadcasted_iota(jnp.int32, (TILE, TILE), 0)
           <= lax.broadcasted_iota(jnp.int32, (TILE, TILE), 1)).astype(BF16)
    ones_sq = jnp.ones((TILE, TILE), BF16)
    c_t = carry_ref[...]
    total = jnp.zeros_like(c_t)
    for part in _split3(lf):
        pb = part.astype(BF16)
        c_t = c_t + _dot(pb, tri)
        total = total + _dot(pb, ones_sq)
    carry_ref[...] = carry_ref[...] + total
    c_hi, c_mid, c_lo = _split3(c_t)

    c_nat = jnp.concatenate([c_t, jnp.zeros((K_WIDTH - LF_ROWS, TILE), F32)], axis=0).T
    parts_nat = jnp.concatenate([p.astype(BF16) for p in _split3(c_nat)], axis=1)
    placed = _dot(parts_nat, place_ref[...])
    k_one = jnp.where((lane_t >= HEAD_DIM + FOX_K_ONE) & (lane_t < HEAD_DIM + FOX_K_ONE + 3), 1.0, 0.0)

    for h in range(N_FOX):
        sl = slice(h * HEAD_DIM, (h + 1) * HEAD_DIM)
        qt_ref[0, h, 0, 0:HEAD_DIM, :] = (tr[sl] * HEAD_DIM ** -0.5).astype(BF16)
        aug = jnp.where(row64 == FOX_Q_C, c_hi[h:h + 1],
              jnp.where(row64 == FOX_Q_C + 1, c_mid[h:h + 1],
              jnp.where(row64 == FOX_Q_C + 2, c_lo[h:h + 1],
              jnp.where(row64 < FOX_Q_C + 6, 1.0, 0.0))))
        qt_ref[0, h, 0, HEAD_DIM:K_WIDTH, :] = aug.astype(BF16)
        vt_ref[0, h, 0, 0:HEAD_DIM, :] = tr[GROUP_WIDTH + h * HEAD_DIM:GROUP_WIDTH + (h + 1) * HEAD_DIM].astype(BF16)
        vt_ref[0, h, 0, HEAD_DIM:V_ROWS, :] = ones_rows
        ksl = slice(h * K_WIDTH, (h + 1) * K_WIDTH)
        k_ref[0, h, 0] = (nat[:, ksl] + placed[:, ksl] + k_one).astype(BF16)

    blk = lax.broadcasted_iota(jnp.int32, (32, TILE), 0)
    r32 = blk
    t_idx = lax.broadcasted_iota(jnp.int32, (32, TILE), 1).astype(F32)
    past = blk < i
    for h in range(N_MOBA):
        hh = N_FOX + h
        slope = _alibi_slope(h)
        q_t = tr[2 * GROUP_WIDTH + h * HEAD_DIM:2 * GROUP_WIDTH + (h + 1) * HEAD_DIM]
        km = kmean_ref[:, (N_FOX + h) * K_WIDTH:(N_FOX + h) * K_WIDTH + HEAD_DIM]
        km_hi, km_mid, _ = _split3(km)
        q_hi, q_mid, _ = _split3(q_t)
        g = (_dot(km_hi.astype(BF16), q_hi.astype(BF16))
             + _dot(km_hi.astype(BF16), q_mid.astype(BF16))
             + _dot(km_mid.astype(BF16), q_hi.astype(BF16)))
        remaining = past
        for _ in range(MOBA_TOPK):
            gm = jnp.where(remaining, g, -jnp.inf)
            mx = jnp.max(gm, axis=0, keepdims=True)
            cand = remaining & (gm == mx)
            first = jnp.min(jnp.where(cand, blk, 32), axis=0, keepdims=True)
            remaining = remaining & (blk != first)
        selected = (past & jnp.logical_not(remaining)) | (blk == i)
        sel_bias = jnp.where(selected, 0.0, NEG_INF)
        tail =jnp.where(r32 == 0, -slope * MOBA_BLOCK * i.astype(F32),
               jnp.where(r32 == 1, -slope * t_idx,
               jnp.where(r32 < 4, 1.0, 0.0)))
        qt_ref[0, hh, 0, 0:HEAD_DIM, :] = (q_t * HEAD_DIM ** -0.5).astype(BF16)
        qt_ref[0, hh, 0, HEAD_DIM:K_WIDTH, :] = jnp.concatenate([sel_bias, tail], axis=0).astype(BF16)
        vt_ref[0, hh, 0, 0:HEAD_DIM, :] = tr[3 * GROUP_WIDTH + h * HEAD_DIM:3 * GROUP_WIDTH + (h + 1) * HEAD_DIM].astype(BF16)
        vt_ref[0, hh, 0, HEAD_DIM:V_ROWS, :] = ones_rows
        a0 = HEAD_DIM + MOBA_ALIBI
        k_aug = jnp.where(lane_t == HEAD_DIM + MOBA_SEL + i, 1.0,
                jnp.where((lane_t == a0) | (lane_t == a0 + 1), 1.0,
                jnp.where(lane_t == a0 + 2, slope * MOBA_BLOCK * i.astype(F32),
                jnp.where(lane_t == a0 + 3, slope * row_t.astype(F32), 0.0))))
        ksl = slice(hh * K_WIDTH, (hh + 1) * K_WIDTH)
        k_ref[0, hh, 0] = (nat[:, ksl] + k_aug).astype(BF16)

    kmean_ref[pl.ds(i, 1), :] = jnp.sum(nat, axis=0, keepdims=True) * (1.0 / MOBA_BLOCK)


def _projection(x, wn, wt, bf, place):
    B, S, _ = x.shape
    n_t = S // TILE
    const = lambda b, i: (0, 0)
    return pl.pallas_call(
        _proj_kernel,
        grid=(B, n_t),
        in_specs=[
            pl.BlockSpec((1, TILE, D_MODEL), lambda b, i: (b, i, 0)),
            pl.BlockSpec(wn.shape, const, pipeline_mode=pl.Buffered(1)),
            pl.BlockSpec(wt.shape, const, pipeline_mode=pl.Buffered(1)),
            pl.BlockSpec(bf.shape, const, pipeline_mode=pl.Buffered(1)),
            pl.BlockSpec(place.shape, const, pipeline_mode=pl.Buffered(1)),
        ],
        out_specs=[
            pl.BlockSpec((1, N_HEADS, 1, TILE, K_WIDTH), lambda b, i: (b, 0, i, 0, 0)),
            pl.BlockSpec((1, N_HEADS, 1, K_WIDTH, TILE), lambda b, i: (b, 0, i, 0, 0)),
            pl.BlockSpec((1, N_HEADS, 1, V_ROWS, TILE), lambda b, i: (b, 0, i, 0, 0)),
        ],
        out_shape=[
            jax.ShapeDtypeStruct((B, N_HEADS, n_t, TILE, K_WIDTH), BF16),
            jax.ShapeDtypeStruct((B, N_HEADS, n_t, K_WIDTH, TILE), BF16),
            jax.ShapeDtypeStruct((B, N_HEADS, n_t, V_ROWS, TILE), BF16),
        ],
        scratch_shapes=[
            pltpu.VMEM((LF_ROWS, TILE), F32),
            pltpu.VMEM((S // MOBA_BLOCK, N_HEADS * K_WIDTH), F32),
        ],
        compiler_params=pltpu.CompilerParams(
            dimension_semantics=("arbitrary", "arbitrary"), vmem_limit_bytes=VMEM_LIMIT),
        name="projection",
    )(x, wn, wt, bf, place)


def _attn_kernel(k_ref, qt_ref, vt_ref, o_ref):
    n_t = qt_ref.shape[2]
    key_idx = lax.broadcasted_iota(jnp.int32, (TILE, TILE), 0)
    qry_idx = lax.broadcasted_iota(jnp.int32, (TILE, TILE), 1)
    causal = key_idx <= qry_idx

    def step(kj, q_t, carry, diagonal):
        m, acc = carry
        s = _dot(k_ref[0, 0, kj], q_t)
        if diagonal:
            s = jnp.where(causal, s, NEG_INF)
        m_new = jnp.maximum(m, jnp.max(s, axis=0, keepdims=True))
        p = jnp.exp(s - m_new).astype(BF16)
        acc = acc * jnp.exp(m - m_new) + _dot(vt_ref[0, 0, kj], p)
        return m_new, acc

    def q_tile(qi, _):
        q_t = qt_ref[0, 0, qi]
        init = (jnp.full((1, TILE), -jnp.inf, F32), jnp.zeros((V_ROWS, TILE), F32))
        carry = lax.fori_loop(0, qi, lambda kj, c: step(kj, q_t, c, False), init)
        _, acc = step(qi, q_t, carry, True)
        o_ref[0, 0, qi] = (acc[0:HEAD_DIM] / acc[HEAD_DIM:HEAD_DIM + 1]).astype(BF16)
        return 0

    lax.fori_loop(0, n_t, q_tile, 0)


def _attention(k, qt, vt):
    B, H, n_t = k.shape[:3]
    spec = lambda r, c: pl.BlockSpec((1, 1, n_t, r, c), lambda b, h: (b, h, 0, 0, 0))
    return pl.pallas_call(
        _attn_kernel,
        grid=(B, H),
        in_specs=[spec(TILE, K_WIDTH), spec(K_WIDTH, TILE), spec(V_ROWS, TILE)],
        out_specs=spec(HEAD_DIM, TILE),
        out_shape=jax.ShapeDtypeStruct((B, H, n_t, HEAD_DIM, TILE), BF16),
        compiler_params=pltpu.CompilerParams(
            dimension_semantics=("arbitrary", "arbitrary"), vmem_limit_bytes=VMEM_LIMIT),
        name="attention",
    )(k, qt, vt)


def _layer_norm(z, g, b):
    mu = jnp.mean(z, axis=-1, keepdims=True)
    d = z - mu
    var = jnp.mean(d * d, axis=-1, keepdims=True)
    return d * lax.rsqrt(var + LN_EPS) * g + b


def _outproj_kernel(ot_ref, x_ref, wo_ref, g_ref, b_ref, out_ref):
    heads_t = ot_ref[0, :, 0].reshape(N_HEADS * HEAD_DIM, TILE)
    y = _dot(wo_ref[...], heads_t).T
    out_ref[0] = _layer_norm(ALPHA * x_ref[0] + y, g_ref[...], b_ref[...])


def _outproj_ln(ot, x, wo_t, g, b):
    B, S, _ = x.shape
    n_t = S // TILE
    const = lambda bb, i: (0, 0)
    return pl.pallas_call(
        _outproj_kernel,
        grid=(B, n_t),
        in_specs=[
            pl.BlockSpec((1, N_HEADS, 1, HEAD_DIM, TILE), lambda bb, i: (bb, 0, i, 0, 0)),
            pl.BlockSpec((1, TILE, D_MODEL), lambda bb, i: (bb, i, 0)),
            pl.BlockSpec(wo_t.shape, const, pipeline_mode=pl.Buffered(1)),
            pl.BlockSpec(g.shape, const),
            pl.BlockSpec(b.shape, const),
        ],
        out_specs=pl.BlockSpec((1, TILE, D_MODEL), lambda bb, i: (bb, i, 0)),
        out_shape=jax.ShapeDtypeStruct(x.shape, F32),
        compiler_params=pltpu.CompilerParams(
            dimension_semantics=("arbitrary", "arbitrary"), vmem_limit_bytes=VMEM_LIMIT),
        name="outproj_ln",
    )(ot, x, wo_t, g, b)


def _ffn_kernel(x_ref, wgu_ref, wd_ref, g_ref, b_ref, out_ref):
    x = x_ref[...]
    gu = _dot(x.astype(BF16), wgu_ref[...])
    gate, up = gu[:, :D_FF], gu[:, D_FF:]
    hidden = (gate * jax.nn.sigmoid(gate) * up).astype(BF16)
    y = _dot(hidden, wd_ref[...])
    out_ref[...] = _layer_norm(ALPHA * x + y, g_ref[...], b_ref[...])


def _ffn_ln(x2d, wgu, wd, g, b):
    n = x2d.shape[0]
    const = lambda i: (0, 0)
    return pl.pallas_call(
        _ffn_kernel,
        grid=(n // TILE,),
        in_specs=[
            pl.BlockSpec((TILE, D_MODEL), lambda i: (i, 0)),
            pl.BlockSpec(wgu.shape, const, pipeline_mode=pl.Buffered(1)),
            pl.BlockSpec(wd.shape, const, pipeline_mode=pl.Buffered(1)),
            pl.BlockSpec(g.shape, const),
            pl.BlockSpec(b.shape, const),
        ],
        out_specs=pl.BlockSpec((TILE, D_MODEL), lambda i: (i, 0)),
        out_shape=jax.ShapeDtypeStruct(x2d.shape, F32),
        compiler_params=pltpu.CompilerParams(
            dimension_semantics=("arbitrary",), vmem_limit_bytes=VMEM_LIMIT),
        name="ffn_ln",
    )(x2d, wgu, wd, g, b)


def _placement_matrix():
    place = np.zeros((3 * K_WIDTH, N_FOX * K_WIDTH), np.float32)
    for p in range(3):
        for h in range(N_FOX):
            place[p * K_WIDTH + h, h * K_WIDTH + HEAD_DIM + FOX_K_ONE + 3 + p] = -1.0
    return jnp.asarray(place, BF16)


def _layout_in_proj(w_in_l, b_f_l):
    fq, fk, fv, ff, mq, mk, mv = jnp.split(
        w_in_l, [GROUP_WIDTH, 2 * GROUP_WIDTH, 3 * GROUP_WIDTH, 3 * GROUP_WIDTH + N_FOX,
                 4 * GROUP_WIDTH + N_FOX, 5 * GROUP_WIDTH + N_FOX], axis=1)

    def pad_heads(w):
        w = w.reshape(D_MODEL, N_FOX, HEAD_DIM)
        return jnp.pad(w, ((0, 0), (0, 0), (0, K_WIDTH - HEAD_DIM))).reshape(D_MODEL, N_FOX * K_WIDTH)

    wn = jnp.concatenate([pad_heads(fk), pad_heads(mk)], axis=1).astype(BF16)
    wt = jnp.concatenate([fq.T, fv.T, mq.T, mv.T, ff.T,
                          jnp.zeros((LF_ROWS - N_FOX, D_MODEL), F32)], axis=0).astype(BF16)
    bf = jnp.broadcast_to(jnp.pad(b_f_l.astype(F32), (0, LF_ROWS - N_FOX))[:, None], (LF_ROWS, TILE))
    return wn, wt, bf


def kernel(x, w_in, b_f, w_o, ln1_g, ln1_b, w_gu, w_down, ln2_g, ln2_b):
    B, S, D = x.shape
    place = _placement_matrix()
    for l in range(DEPTH):
        wn, wt, bf = _layout_in_proj(w_in[l], b_f[l])
        k, qt, vt = _projection(x, wn, wt, bf, place)
        ot = _attention(k, qt, vt)
        x = _outproj_ln(ot, x, w_o[l].T.astype(BF16), ln1_g[l][None, :], ln1_b[l][None, :])
        x = _ffn_ln(x.reshape(B * S, D), w_gu[l].astype(BF16), w_down[l].astype(BF16),
                    ln2_g[l][None, :], ln2_b[l][None, :]).reshape(B, S, D)
    return x
```

```python
import functools

import jax
import jax.numpy as jnp
import numpy as np
from jax import lax
from jax.experimental import pallas as pl
from jax.experimental.pallas import tpu as pltpu

D_MODEL = 1024
HEAD_DIM = 64
N_FOX = 8
N_MOBA = 8
N_HEADS = N_FOX + N_MOBA
GROUP_WIDTH = N_FOX * HEAD_DIM
MOBA_BLOCK = 256
MOBA_TOPK = 3
D_FF = 2816
DEPTH = 2
ALPHA = (2 * DEPTH) ** 0.25
LN_EPS = 1e-5
NEG_INF = -1e30

TILE = 256
K_WIDTH = 128
AUG = K_WIDTH - HEAD_DIM
V_ROWS = 80
LF_ROWS = 16
VMEM_LIMIT = 56 * 1024 * 1024
HEADS_PER_STEP = 8

F32 = jnp.float32
BF16 = jnp.bfloat16

FOX_Q_C = 0
FOX_K_ONE = 0
MOBA_SEL = 0
MOBA_ALIBI = 32


def _dot(a, b):
    return jnp.dot(a, b, preferred_element_type=F32)


def _dot_nt(a, b):
    return lax.dot_general(a, b, (((1,), (1,)), ((), ())), preferred_element_type=F32)


def _split3(x):
    hi = x.astype(BF16).astype(F32)
    r = x - hi
    mid = r.astype(BF16).astype(F32)
    lo = (r - mid).astype(BF16).astype(F32)
    return hi, mid, lo


def _log_sigmoid(x):
    return jnp.minimum(x, 0.0) - jnp.log1p(jnp.exp(-jnp.abs(x)))


def _alibi_slope(h):
    return 2.0 ** (-8.0 * (h + 1) / N_MOBA)


def _proj_kernel(x_ref, wn_ref, wt_ref, bf_ref, place_ref, k_ref, qt_ref, vt_ref,
                 carry_ref, kmean_ref):
    i = pl.program_id(1)

    @pl.when(i == 0)
    def _():
        carry_ref[...] = jnp.zeros_like(carry_ref)
        kmean_ref[...] = jnp.zeros_like(kmean_ref)

    xb = x_ref[0].astype(BF16)
    nat = _dot(xb, wn_ref[...])
    tr = _dot_nt(wt_ref[...], xb)

    row64 = lax.broadcasted_iota(jnp.int32, (AUG, TILE), 0)
    row_t = lax.broadcasted_iota(jnp.int32, (TILE, K_WIDTH), 0)
    lane_t = lax.broadcasted_iota(jnp.int32, (TILE, K_WIDTH), 1)
    ones_rows = jnp.where(lax.broadcasted_iota(jnp.int32, (V_ROWS - HEAD_DIM, TILE), 0) == 0,
                          1.0, 0.0).astype(BF16)

    lf = _log_sigmoid(tr[4 * GROUP_WIDTH:4 * GROUP_WIDTH + LF_ROWS] + bf_ref[...])
    tri = (lax.broadcasted_iota(jnp.int32, (TILE, TILE), 0)
           <= lax.broadcasted_iota(jnp.int32, (TILE, TILE), 1)).astype(BF16)
    ones_sq = jnp.ones((TILE, TILE), BF16)
    c_t = carry_ref[...]
    total = jnp.zeros_like(c_t)
    for part in _split3(lf):
        pb = part.astype(BF16)
        c_t = c_t + _dot(pb, tri)
        total = total + _dot(pb, ones_sq)
    carry_ref[...] = carry_ref[...] + total
    c_hi, c_mid, c_lo = _split3(c_t)

    c_nat = jnp.concatenate([c_t, jnp.zeros((K_WIDTH - LF_ROWS, TILE), F32)], axis=0).T
    parts_nat = jnp.concatenate([p.astype(BF16) for p in _split3(c_nat)], axis=1)
    placed = _dot(parts_nat, place_ref[...])
    k_one = jnp.where((lane_t >= HEAD_DIM + FOX_K_ONE) & (lane_t < HEAD_DIM + FOX_K_ONE + 3), 1.0, 0.0)

    for h in range(N_FOX):
        sl = slice(h * HEAD_DIM, (h + 1) * HEAD_DIM)
        qt_ref[0, h, 0, 0:HEAD_DIM, :] = (tr[sl] * HEAD_DIM ** -0.5).astype(BF16)
        aug = jnp.where(row64 == FOX_Q_C, c_hi[h:h + 1],
              jnp.where(row64 == FOX_Q_C + 1, c_mid[h:h + 1],
              jnp.where(row64 == FOX_Q_C + 2, c_lo[h:h + 1],
              jnp.where(row64 < FOX_Q_C + 6, 1.0, 0.0))))
        qt_ref[0, h, 0, HEAD_DIM:K_WIDTH, :] = aug.astype(BF16)
        vt_ref[0, h, 0, 0:HEAD_DIM, :] = tr[GROUP_WIDTH + h * HEAD_DIM:GROUP_WIDTH + (h + 1) * HEAD_DIM].astype(BF16)
        vt_ref[0, h, 0, HEAD_DIM:V_ROWS, :] = ones_rows
        ksl = slice(h * K_WIDTH, (h + 1) * K_WIDTH)
        k_ref[0, h, 0] = (nat[:, ksl] + placed[:, ksl] + k_one).astype(BF16)

    blk = lax.broadcasted_iota(jnp.int32, (32, TILE), 0)
    r32 = blk
    t_idx = lax.broadcasted_iota(jnp.int32, (32, TILE), 1).astype(F32)
    past = blk < i
    for h in range(N_MOBA):
        hh = N_FOX + h
        slope = _alibi_slope(h)
        q_t = tr[2 * GROUP_WIDTH + h * HEAD_DIM:2 * GROUP_WIDTH + (h + 1) * HEAD_DIM]
        km = kmean_ref[:, (N_FOX + h) * K_WIDTH:(N_FOX + h) * K_WIDTH + HEAD_DIM]
        km_hi, km_mid, _ = _split3(km)
        q_hi, q_mid, _ = _split3(q_t)
        g = (_dot(km_hi.astype(BF16), q_hi.astype(BF16))
             + _dot(km_hi.astype(BF16), q_mid.astype(BF16))
             + _dot(km_mid.astype(BF16), q_hi.astype(BF16)))
        remaining = past
        for _ in range(MOBA_TOPK):
            gm = jnp.where(remaining, g, -jnp.inf)
            mx = jnp.max(gm, axis=0, keepdims=True)
            cand = remaining & (gm == mx)
            first = jnp.min(jnp.where(cand, blk, 32), axis=0, keepdims=True)
            remaining = remaining & (blk != first)
        selected = (past & jnp.logical_not(remaining)) | (blk == i)
        sel_bias = jnp.where(selected, 0.0, NEG_INF)
        tail =jnp.where(r32 == 0, -slope * MOBA_BLOCK * i.astype(F32),
               jnp.where(r32 == 1, -slope * t_idx,
               jnp.where(r32 < 4, 1.0, 0.0)))
        qt_ref[0, hh, 0, 0:HEAD_DIM, :] = (q_t * HEAD_DIM ** -0.5).astype(BF16)
        qt_ref[0, hh, 0, HEAD_DIM:K_WIDTH, :] = jnp.concatenate([sel_bias, tail], axis=0).astype(BF16)
        vt_ref[0, hh, 0, 0:HEAD_DIM, :] = tr[3 * GROUP_WIDTH + h * HEAD_DIM:3 * GROUP_WIDTH + (h + 1) * HEAD_DIM].astype(BF16)
        vt_ref[0, hh, 0, HEAD_DIM:V_ROWS, :] = ones_rows
        a0 = HEAD_DIM + MOBA_ALIBI
        k_aug = jnp.where(lane_t == HEAD_DIM + MOBA_SEL + i, 1.0,
                jnp.where((lane_t == a0) | (lane_t == a0 + 1), 1.0,
                jnp.where(lane_t == a0 + 2, slope * MOBA_BLOCK * i.astype(F32),
                jnp.where(lane_t == a0 + 3, slope * row_t.astype(F32), 0.0))))
        ksl = slice(hh * K_WIDTH, (hh + 1) * K_WIDTH)
        k_ref[0, hh, 0] = (nat[:, ksl] + k_aug).astype(BF16)

    kmean_ref[pl.ds(i, 1), :] = jnp.sum(nat, axis=0, keepdims=True) * (1.0 / MOBA_BLOCK)


def _projection(x, wn, wt, bf, place):
    B, S, _ = x.shape
    n_t = S // TILE
    const = lambda b, i: (0, 0)
    return pl.pallas_call(
        _proj_kernel,
        grid=(B, n_t),
        in_specs=[
            pl.BlockSpec((1, TILE, D_MODEL), lambda b, i: (b, i, 0)),
            pl.BlockSpec(wn.shape, const, pipeline_mode=pl.Buffered(1)),
            pl.BlockSpec(wt.shape, const, pipeline_mode=pl.Buffered(1)),
            pl.BlockSpec(bf.shape, const, pipeline_mode=pl.Buffered(1)),
            pl.BlockSpec(place.shape, const, pipeline_mode=pl.Buffered(1)),
        ],
        out_specs=[
            pl.BlockSpec((1, N_HEADS, 1, TILE, K_WIDTH), lambda b, i: (b, 0, i, 0, 0)),
            pl.BlockSpec((1, N_HEADS, 1, K_WIDTH, TILE), lambda b, i: (b, 0, i, 0, 0)),
            pl.BlockSpec((1, N_HEADS, 1, V_ROWS, TILE), lambda b, i: (b, 0, i, 0, 0)),
        ],
        out_shape=[
            jax.ShapeDtypeStruct((B, N_HEADS, n_t, TILE, K_WIDTH), BF16),
            jax.ShapeDtypeStruct((B, N_HEADS, n_t, K_WIDTH, TILE), BF16),
            jax.ShapeDtypeStruct((B, N_HEADS, n_t, V_ROWS, TILE), BF16),
        ],
        scratch_shapes=[
            pltpu.VMEM((LF_ROWS, TILE), F32),
            pltpu.VMEM((S // MOBA_BLOCK, N_HEADS * K_WIDTH), F32),
        ],
        compiler_params=pltpu.CompilerParams(
            dimension_semantics=("arbitrary", "arbitrary"), vmem_limit_bytes=VMEM_LIMIT),
        name="projection",
    )(x, wn, wt, bf, place)


def _attn_kernel(k_ref, qt_ref, vt_ref, o_ref):
    qi = pl.program_id(2)
    n_h = qt_ref.shape[1]
    key_idx = lax.broadcasted_iota(jnp.int32, (TILE, TILE), 0)
    qry_idx = lax.broadcasted_iota(jnp.int32, (TILE, TILE), 1)
    causal = key_idx <= qry_idx

    def step(kj, carry, diagonal):
        logits = [_dot(k_ref[0, h, kj], qt_ref[0, h, 0]) for h in range(n_h)]
        out = []
        for h in range(n_h):
            m, acc = carry[h]
            s = logits[h]
            if diagonal:
                s = jnp.where(causal, s, NEG_INF)
            m_new = jnp.maximum(m, jnp.max(s, axis=0, keepdims=True))
            p = jnp.exp(s - m_new).astype(BF16)
            acc = acc * jnp.exp(m - m_new) + _dot(vt_ref[0, h, kj], p)
            out.append((m_new, acc))
        return tuple(out)

    init = tuple((jnp.full((1, TILE), -jnp.inf, F32), jnp.zeros((V_ROWS, TILE), F32)) for _ in range(n_h))
    carry = lax.fori_loop(0, qi, lambda kj, c: step(kj, c, False), init)
    carry = step(qi, carry, True)
    for h in range(n_h):
        acc = carry[h][1]
        o_ref[0, h, 0] = (acc[0:HEAD_DIM] / acc[HEAD_DIM:HEAD_DIM + 1]).astype(BF16)


def _attention(k, qt, vt):
    B, H, n_t = k.shape[:3]
    n_groups = H // HEADS_PER_STEP
    resident = lambda r, c: pl.BlockSpec((1, HEADS_PER_STEP, n_t, r, c), lambda b, g, i: (b, g, 0, 0, 0),
                                         pipeline_mode=pl.Buffered(1))
    tiled = lambda r, c: pl.BlockSpec((1, HEADS_PER_STEP, 1, r, c), lambda b, g, i: (b, g, i, 0, 0))
    return pl.pallas_call(
        _attn_kernel,
        grid=(B, n_groups, n_t),
        in_specs=[resident(TILE, K_WIDTH), tiled(K_WIDTH, TILE), resident(V_ROWS, TILE)],
        out_specs=tiled(HEAD_DIM, TILE),
        out_shape=jax.ShapeDtypeStruct((B, H, n_t, HEAD_DIM, TILE), BF16),
        compiler_params=pltpu.CompilerParams(
            dimension_semantics=("arbitrary", "arbitrary", "arbitrary"), vmem_limit_bytes=VMEM_LIMIT),
        name="attention",
    )(k, qt, vt)


def _layer_norm(z, g, b):
    mu = jnp.mean(z, axis=-1, keepdims=True)
    d = z - mu
    var = jnp.mean(d * d, axis=-1, keepdims=True)
    return d * lax.rsqrt(var + LN_EPS) * g + b


def _outproj_kernel(ot_ref, x_ref, wo_ref, g_ref, b_ref, out_ref):
    heads_t = ot_ref[0, :, 0].reshape(N_HEADS * HEAD_DIM, TILE)
    y = _dot(wo_ref[...], heads_t).T
    out_ref[0] = _layer_norm(ALPHA * x_ref[0] + y, g_ref[...], b_ref[...])


def _outproj_ln(ot, x, wo_t, g, b):
    B, S, _ = x.shape
    n_t = S // TILE
    const = lambda bb, i: (0, 0)
    return pl.pallas_call(
        _outproj_kernel,
        grid=(B, n_t),
        in_specs=[
            pl.BlockSpec((1, N_HEADS, 1, HEAD_DIM, TILE), lambda bb, i: (bb, 0, i, 0, 0)),
            pl.BlockSpec((1, TILE, D_MODEL), lambda bb, i: (bb, i, 0)),
            pl.BlockSpec(wo_t.shape, const, pipeline_mode=pl.Buffered(1)),
            pl.BlockSpec(g.shape, const),
            pl.BlockSpec(b.shape, const),
        ],
        out_specs=pl.BlockSpec((1, TILE, D_MODEL), lambda bb, i: (bb, i, 0)),
        out_shape=jax.ShapeDtypeStruct(x.shape, F32),
        compiler_params=pltpu.CompilerParams(
            dimension_semantics=("arbitrary", "arbitrary"), vmem_limit_bytes=VMEM_LIMIT),
        name="outproj_ln",
    )(ot, x, wo_t, g, b)


def _ffn_kernel(x_ref, wgu_ref, wd_ref, g_ref, b_ref, out_ref):
    x = x_ref[...]
    gu = _dot(x.astype(BF16), wgu_ref[...])
    gate, up = gu[:, :D_FF], gu[:, D_FF:]
    hidden = (gate * jax.nn.sigmoid(gate) * up).astype(BF16)
    y = _dot(hidden, wd_ref[...])
    out_ref[...] = _layer_norm(ALPHA * x + y, g_ref[...], b_ref[...])


def _ffn_ln(x2d, wgu, wd, g, b):
    n = x2d.shape[0]
    const = lambda i: (0, 0)
    return pl.pallas_call(
        _ffn_kernel,
        grid=(n // TILE,),
        in_specs=[
            pl.BlockSpec((TILE, D_MODEL), lambda i: (i, 0)),
            pl.BlockSpec(wgu.shape, const, pipeline_mode=pl.Buffered(1)),
            pl.BlockSpec(wd.shape, const, pipeline_mode=pl.Buffered(1)),
            pl.BlockSpec(g.shape, const),
            pl.BlockSpec(b.shape, const),
        ],
        out_specs=pl.BlockSpec((TILE, D_MODEL), lambda i: (i, 0)),
        out_shape=jax.ShapeDtypeStruct(x2d.shape, F32),
        compiler_params=pltpu.CompilerParams(
            dimension_semantics=("arbitrary",), vmem_limit_bytes=VMEM_LIMIT),
        name="ffn_ln",
    )(x2d, wgu, wd, g, b)


def _placement_matrix():
    place = np.zeros((3 * K_WIDTH, N_FOX * K_WIDTH), np.float32)
    for p in range(3):
        for h in range(N_FOX):
            place[p * K_WIDTH + h, h * K_WIDTH + HEAD_DIM + FOX_K_ONE + 3 + p] = -1.0
    return jnp.asarray(place, BF16)


def _layout_in_proj(w_in_l, b_f_l):
    fq, fk, fv, ff, mq, mk, mv = jnp.split(
        w_in_l, [GROUP_WIDTH, 2 * GROUP_WIDTH, 3 * GROUP_WIDTH, 3 * GROUP_WIDTH + N_FOX,
                 4 * GROUP_WIDTH + N_FOX, 5 * GROUP_WIDTH + N_FOX], axis=1)

    def pad_heads(w):
        w = w.reshape(D_MODEL, N_FOX, HEAD_DIM)
        return jnp.pad(w, ((0, 0), (0, 0), (0, K_WIDTH - HEAD_DIM))).reshape(D_MODEL, N_FOX * K_WIDTH)

    wn = jnp.concatenate([pad_heads(fk), pad_heads(mk)], axis=1).astype(BF16)
    wt = jnp.concatenate([fq.T, fv.T, mq.T, mv.T, ff.T,
                          jnp.zeros((LF_ROWS - N_FOX, D_MODEL), F32)], axis=0).astype(BF16)
    bf = jnp.broadcast_to(jnp.pad(b_f_l.astype(F32), (0, LF_ROWS - N_FOX))[:, None], (LF_ROWS, TILE))
    return wn, wt, bf


def kernel(x, w_in, b_f, w_o, ln1_g, ln1_b, w_gu, w_down, ln2_g, ln2_b):
    B, S, D = x.shape
    place = _placement_matrix()
    for l in range(DEPTH):
        wn, wt, bf = _layout_in_proj(w_in[l], b_f[l])
        k, qt, vt = _projection(x, wn, wt, bf, place)
        ot = _attention(k, qt, vt)
        x = _outproj_ln(ot, x, w_o[l].T.astype(BF16), ln1_g[l][None, :], ln1_b[l][None, :])
        x = _ffn_ln(x.reshape(B * S, D), w_gu[l].astype(BF16), w_down[l].astype(BF16),
                    ln2_g[l][None, :], ln2_b[l][None, :]).reshape(B, S, D)
    return x
```

```python
import functools

import jax
import jax.numpy as jnp
import numpy as np
from jax import lax
from jax.experimental import pallas as pl
from jax.experimental.pallas import tpu as pltpu

D_MODEL = 1024
HEAD_DIM = 64
N_FOX = 8
N_MOBA = 8
N_HEADS = N_FOX + N_MOBA
GROUP_WIDTH = N_FOX * HEAD_DIM
MOBA_BLOCK = 256
MOBA_TOPK = 3
D_FF = 2816
DEPTH = 2
ALPHA = (2 * DEPTH) ** 0.25
LN_EPS = 1e-5
NEG_INF = -1e30

TILE = 256
Q_TILE = 2 * TILE
K_WIDTH = 128
AUG = K_WIDTH - HEAD_DIM
V_ROWS = 80
LF_ROWS = 16
VMEM_LIMIT = 56 * 1024 * 1024
HEADS_PER_STEP = 8

F32 = jnp.float32
BF16 = jnp.bfloat16

FOX_Q_C = 0
FOX_K_ONE = 0
MOBA_SEL = 0
MOBA_ALIBI = 32


def _dot(a, b):
    return jnp.dot(a, b, preferred_element_type=F32)


def _dot_nt(a, b):
    return lax.dot_general(a, b, (((1,), (1,)), ((), ())), preferred_element_type=F32)


def _split3(x):
    hi = x.astype(BF16).astype(F32)
    r = x - hi
    mid = r.astype(BF16).astype(F32)
    lo = (r - mid).astype(BF16).astype(F32)
    return hi, mid, lo


def _log_sigmoid(x):
    return jnp.minimum(x, 0.0) - jnp.log1p(jnp.exp(-jnp.abs(x)))


def _alibi_slope(h):
    return 2.0 ** (-8.0 * (h + 1) / N_MOBA)


def _proj_kernel(x_ref, wn_ref, wt_ref, bf_ref, place_ref, k_ref, qt_ref, vt_ref,
                 carry_ref, kmean_ref):
    i = pl.program_id(1)

    @pl.when(i == 0)
    def _():
        carry_ref[...] = jnp.zeros_like(carry_ref)
        kmean_ref[...] = jnp.zeros_like(kmean_ref)

    xb = x_ref[0].astype(BF16)
    nat = _dot(xb, wn_ref[...])
    tr = _dot_nt(wt_ref[...], xb)

    row64 = lax.broadcasted_iota(jnp.int32, (AUG, TILE), 0)
    row_t = lax.broadcasted_iota(jnp.int32, (TILE, K_WIDTH), 0)
    lane_t = lax.broadcasted_iota(jnp.int32, (TILE, K_WIDTH), 1)
    ones_rows = jnp.where(lax.broadcasted_iota(jnp.int32, (V_ROWS - HEAD_DIM, TILE), 0) == 0,
                          1.0, 0.0).astype(BF16)

    lf = _log_sigmoid(tr[4 * GROUP_WIDTH:4 * GROUP_WIDTH + LF_ROWS] + bf_ref[...])
    tri = (lax.broadcasted_iota(jnp.int32, (TILE, TILE), 0)
           <= lax.broadcasted_iota(jnp.int32, (TILE, TILE), 1)).astype(BF16)
    ones_sq = jnp.ones((TILE, TILE), BF16)
    c_t = carry_ref[...]
    total = jnp.zeros_like(c_t)
    for part in _split3(lf):
        pb = part.astype(BF16)
        c_t = c_t + _dot(pb, tri)
        total = total + _dot(pb, ones_sq)
    carry_ref[...] = carry_ref[...] + total
    c_hi, c_mid, c_lo = _split3(c_t)

    c_nat = jnp.concatenate([c_t, jnp.zeros((K_WIDTH - LF_ROWS, TILE), F32)], axis=0).T
    parts_nat = jnp.concatenate([p.astype(BF16) for p in _split3(c_nat)], axis=1)
    placed = _dot(parts_nat, place_ref[...])
    k_one = jnp.where((lane_t >= HEAD_DIM + FOX_K_ONE) & (lane_t < HEAD_DIM + FOX_K_ONE + 3), 1.0, 0.0)

    for h in range(N_FOX):
        sl = slice(h * HEAD_DIM, (h + 1) * HEAD_DIM)
        qt_ref[0, h, 0, 0:HEAD_DIM, :] = (tr[sl] * HEAD_DIM ** -0.5).astype(BF16)
        aug = jnp.where(row64 == FOX_Q_C, c_hi[h:h + 1],
              jnp.where(row64 == FOX_Q_C + 1, c_mid[h:h + 1],
              jnp.where(row64 == FOX_Q_C + 2, c_lo[h:h + 1],
              jnp.where(row64 < FOX_Q_C + 6, 1.0, 0.0))))
        qt_ref[0, h, 0, HEAD_DIM:K_WIDTH, :] = aug.astype(BF16)
        vt_ref[0, h, 0, 0:HEAD_DIM, :] = tr[GROUP_WIDTH + h * HEAD_DIM:GROUP_WIDTH + (h + 1) * HEAD_DIM].astype(BF16)
        vt_ref[0, h, 0, HEAD_DIM:V_ROWS, :] = ones_rows
        ksl = slice(h * K_WIDTH, (h + 1) * K_WIDTH)
        k_ref[0, h, 0] = (nat[:, ksl] + placed[:, ksl] + k_one).astype(BF16)

    blk = lax.broadcasted_iota(jnp.int32, (32, TILE), 0)
    r32 = blk
    t_idx = lax.broadcasted_iota(jnp.int32, (32, TILE), 1).astype(F32)
    past = blk < i
    for h in range(N_MOBA):
        hh = N_FOX + h
        slope = _alibi_slope(h)
        q_t = tr[2 * GROUP_WIDTH + h * HEAD_DIM:2 * GROUP_WIDTH + (h + 1) * HEAD_DIM]
        km = kmean_ref[:, (N_FOX + h) * K_WIDTH:(N_FOX + h) * K_WIDTH + HEAD_DIM]
        km_hi, km_mid, _ = _split3(km)
        q_hi, q_mid, _ = _split3(q_t)
        g = (_dot(km_hi.astype(BF16), q_hi.astype(BF16))
             + _dot(km_hi.astype(BF16), q_mid.astype(BF16))
             + _dot(km_mid.astype(BF16), q_hi.astype(BF16)))
        remaining = past
        for _ in range(MOBA_TOPK):
            gm = jnp.where(remaining, g, -jnp.inf)
            mx = jnp.max(gm, axis=0, keepdims=True)
            cand = remaining & (gm == mx)
            first = jnp.min(jnp.where(cand, blk, 32), axis=0, keepdims=True)
            remaining = remaining & (blk != first)
        selected = (past & jnp.logical_not(remaining)) | (blk == i)
        sel_bias = jnp.where(selected, 0.0, NEG_INF)
        tail =jnp.where(r32 == 0, -slope * MOBA_BLOCK * i.astype(F32),
               jnp.where(r32 == 1, -slope * t_idx,
               jnp.where(r32 < 4, 1.0, 0.0)))
        qt_ref[0, hh, 0, 0:HEAD_DIM, :] = (q_t * HEAD_DIM ** -0.5).astype(BF16)
        qt_ref[0, hh, 0, HEAD_DIM:K_WIDTH, :] = jnp.concatenate([sel_bias, tail], axis=0).astype(BF16)
        vt_ref[0, hh, 0, 0:HEAD_DIM, :] = tr[3 * GROUP_WIDTH + h * HEAD_DIM:3 * GROUP_WIDTH + (h + 1) * HEAD_DIM].astype(BF16)
        vt_ref[0, hh, 0, HEAD_DIM:V_ROWS, :] = ones_rows
        a0 = HEAD_DIM + MOBA_ALIBI
        k_aug = jnp.where(lane_t == HEAD_DIM + MOBA_SEL + i, 1.0,
                jnp.where((lane_t == a0) | (lane_t == a0 + 1), 1.0,
                jnp.where(lane_t == a0 + 2, slope * MOBA_BLOCK * i.astype(F32),
                jnp.where(lane_t == a0 + 3, slope * row_t.astype(F32), 0.0))))
        ksl = slice(hh * K_WIDTH, (hh + 1) * K_WIDTH)
        k_ref[0, hh, 0] = (nat[:, ksl] + k_aug).astype(BF16)

    kmean_ref[pl.ds(i, 1), :] = jnp.sum(nat, axis=0, keepdims=True) * (1.0 / MOBA_BLOCK)


def _projection(x, wn, wt, bf, place):
    B, S, _ = x.shape
    n_t = S // TILE
    q_per = Q_TILE // TILE
    const = lambda b, i: (0, 0)
    return pl.pallas_call(
        _proj_kernel,
        grid=(B, n_t),
        in_specs=[
            pl.BlockSpec((1, TILE, D_MODEL), lambda b, i: (b, i, 0)),
            pl.BlockSpec(wn.shape, const, pipeline_mode=pl.Buffered(1)),
            pl.BlockSpec(wt.shape, const, pipeline_mode=pl.Buffered(1)),
            pl.BlockSpec(bf.shape, const, pipeline_mode=pl.Buffered(1)),
            pl.BlockSpec(place.shape, const, pipeline_mode=pl.Buffered(1)),
        ],
        out_specs=[
            pl.BlockSpec((1, N_HEADS, 1, TILE, K_WIDTH), lambda b, i: (b, 0, i, 0, 0)),
            pl.BlockSpec((1, N_HEADS, 1, K_WIDTH, TILE), lambda b, i: (b, 0, i // q_per, 0, i % q_per)),
            pl.BlockSpec((1, N_HEADS, 1, V_ROWS, TILE), lambda b, i: (b, 0, i, 0, 0)),
        ],
        out_shape=[
            jax.ShapeDtypeStruct((B, N_HEADS, n_t, TILE, K_WIDTH), BF16),
            jax.ShapeDtypeStruct((B, N_HEADS, S // Q_TILE, K_WIDTH, Q_TILE), BF16),
            jax.ShapeDtypeStruct((B, N_HEADS, n_t, V_ROWS, TILE), BF16),
        ],
        scratch_shapes=[
            pltpu.VMEM((LF_ROWS, TILE), F32),
            pltpu.VMEM((S // MOBA_BLOCK, N_HEADS * K_WIDTH), F32),
        ],
        compiler_params=pltpu.CompilerParams(
            dimension_semantics=("arbitrary", "arbitrary"), vmem_limit_bytes=VMEM_LIMIT),
        name="projection",
    )(x, wn, wt, bf, place)


def _attn_kernel(k_ref, qt_ref, vt_ref, o_ref, s_ref, tmax_ref, m_ref, acc_ref):
    qi = pl.program_id(2)
    n_h = qt_ref.shape[1]
    tiles_per_q = Q_TILE // TILE
    first_masked = tiles_per_q * qi
    m_ref[...] = jnp.full(m_ref.shape, -jnp.inf, F32)
    acc_ref[...] = jnp.zeros(acc_ref.shape, F32)

    def logits(h, kj, slot, masked):
        s = _dot(k_ref[0, h, kj], qt_ref[0, h, 0])
        if masked:
            key_idx = lax.broadcasted_iota(jnp.int32, (TILE, Q_TILE), 0)
            qry_idx = lax.broadcasted_iota(jnp.int32, (TILE, Q_TILE), 1)
            s = jnp.where(key_idx + (kj - first_masked) * TILE <= qry_idx, s, NEG_INF)
        s_ref[slot, h] = s
        tmax_ref[slot, h] = jnp.max(s, axis=0, keepdims=True)

    def accumulate(h, kj, slot):
        m = m_ref[h]
        m_new = jnp.maximum(m, tmax_ref[slot, h])
        p = jnp.exp(s_ref[slot, h] - m_new).astype(BF16)
        acc_ref[h] = acc_ref[h] * jnp.exp(m - m_new) + _dot(vt_ref[0, h, kj], p)
        m_ref[h] = m_new

    def pair(kj, slot, masked):
        for h in range(n_h):
            logits(h, kj + 1, 1 - slot, masked)
            accumulate(h, kj, slot)

    for h in range(n_h):
        logits(h, 0, 0, True)

    @pl.when(qi > 0)
    def _():
        pair(0, 0, False)

        def two_pairs(i, c):
            pair(2 * i + 1, 1, False)
            pair(2 * i + 2, 0, False)
            return c
        lax.fori_loop(0, qi - 1, two_pairs, 0)
        pair(first_masked - 1, 1, True)

    pair(first_masked, 0, True)
    for h in range(n_h):
        accumulate(h, first_masked + 1, 1)
    for h in range(n_h):
        acc = acc_ref[h]
        o_ref[0, h, 0] = (acc[0:HEAD_DIM] / acc[HEAD_DIM:HEAD_DIM + 1]).astype(BF16)


def _attention(k, qt, vt):
    B, H, n_t = k.shape[:3]
    n_q = qt.shape[2]
    hps = HEADS_PER_STEP
    resident = lambda r, c: pl.BlockSpec((1, hps, n_t, r, c), lambda b, g, i: (b, g, 0, 0, 0),
                                         pipeline_mode=pl.Buffered(1))
    tiled = lambda r, c: pl.BlockSpec((1, hps, 1, r, c), lambda b, g, i: (b, g, i, 0, 0))
    return pl.pallas_call(
        _attn_kernel,
        grid=(B, H // hps, n_q),
        in_specs=[resident(TILE, K_WIDTH), tiled(K_WIDTH, Q_TILE), resident(V_ROWS, TILE)],
        out_specs=tiled(HEAD_DIM, Q_TILE),
        out_shape=jax.ShapeDtypeStruct((B, H, n_q, HEAD_DIM, Q_TILE), BF16),
        scratch_shapes=[
            pltpu.VMEM((2, hps, TILE, Q_TILE), F32),
            pltpu.VMEM((2, hps, 1, Q_TILE), F32),
            pltpu.VMEM((hps, 1, Q_TILE), F32),
            pltpu.VMEM((hps, V_ROWS, Q_TILE), F32),
        ],
        compiler_params=pltpu.CompilerParams(
            dimension_semantics=("arbitrary", "arbitrary", "arbitrary"), vmem_limit_bytes=VMEM_LIMIT),
        name="attention",
    )(k, qt, vt)


def _layer_norm(z, g, b):
    mu = jnp.mean(z, axis=-1, keepdims=True)
    d = z - mu
    var = jnp.mean(d * d, axis=-1, keepdims=True)
    return d * lax.rsqrt(var + LN_EPS) * g + b


def _outproj_kernel(ot_ref, x_ref, wo_ref, g_ref, b_ref, out_ref):
    heads_t = ot_ref[0, :, 0].reshape(N_HEADS * HEAD_DIM, TILE)
    y = _dot(wo_ref[...], heads_t).T
    out_ref[0] = _layer_norm(ALPHA * x_ref[0] + y, g_ref[...], b_ref[...])


def _outproj_ln(ot, x, wo_t, g, b):
    B, S, _ = x.shape
    n_t = S // TILE
    q_per = Q_TILE // TILE
    const = lambda bb, i: (0, 0)
    return pl.pallas_call(
        _outproj_kernel,
        grid=(B, n_t),
        in_specs=[
            pl.BlockSpec((1, N_HEADS, 1, HEAD_DIM, TILE), lambda bb, i: (bb, 0, i // q_per, 0, i % q_per)),
            pl.BlockSpec((1, TILE, D_MODEL), lambda bb, i: (bb, i, 0)),
            pl.BlockSpec(wo_t.shape, const, pipeline_mode=pl.Buffered(1)),
            pl.BlockSpec(g.shape, const),
            pl.BlockSpec(b.shape, const),
        ],
        out_specs=pl.BlockSpec((1, TILE, D_MODEL), lambda bb, i: (bb, i, 0)),
        out_shape=jax.ShapeDtypeStruct(x.shape, F32),
        compiler_params=pltpu.CompilerParams(
            dimension_semantics=("arbitrary", "arbitrary"), vmem_limit_bytes=VMEM_LIMIT),
        name="outproj_ln",
    )(ot, x, wo_t, g, b)


def _ffn_kernel(x_ref, wgu_ref, wd_ref, g_ref, b_ref, out_ref):
    x = x_ref[...]
    gu = _dot(x.astype(BF16), wgu_ref[...])
    gate, up = gu[:, :D_FF], gu[:, D_FF:]
    hidden = (gate * jax.nn.sigmoid(gate) * up).astype(BF16)
    y = _dot(hidden, wd_ref[...])
    out_ref[...] = _layer_norm(ALPHA * x + y, g_ref[...], b_ref[...])


def _ffn_ln(x2d, wgu, wd, g, b):
    n = x2d.shape[0]
    const = lambda i: (0, 0)
    return pl.pallas_call(
        _ffn_kernel,
        grid=(n // TILE,),
        in_specs=[
            pl.BlockSpec((TILE, D_MODEL), lambda i: (i, 0)),
            pl.BlockSpec(wgu.shape, const, pipeline_mode=pl.Buffered(1)),
            pl.BlockSpec(wd.shape, const, pipeline_mode=pl.Buffered(1)),
            pl.BlockSpec(g.shape, const),
            pl.BlockSpec(b.shape, const),
        ],
        out_specs=pl.BlockSpec((TILE, D_MODEL), lambda i: (i, 0)),
        out_shape=jax.ShapeDtypeStruct(x2d.shape, F32),
        compiler_params=pltpu.CompilerParams(
            dimension_semantics=("arbitrary",), vmem_limit_bytes=VMEM_LIMIT),
        name="ffn_ln",
    )(x2d, wgu, wd, g, b)


def _placement_matrix():
    place = np.zeros((3 * K_WIDTH, N_FOX * K_WIDTH), np.float32)
    for p in range(3):
        for h in range(N_FOX):
            place[p * K_WIDTH + h, h * K_WIDTH + HEAD_DIM + FOX_K_ONE + 3 + p] = -1.0
    return jnp.asarray(place, BF16)


def _layout_in_proj(w_in_l, b_f_l):
    fq, fk, fv, ff, mq, mk, mv = jnp.split(
        w_in_l, [GROUP_WIDTH, 2 * GROUP_WIDTH, 3 * GROUP_WIDTH, 3 * GROUP_WIDTH + N_FOX,
                 4 * GROUP_WIDTH + N_FOX, 5 * GROUP_WIDTH + N_FOX], axis=1)

    def pad_heads(w):
        w = w.reshape(D_MODEL, N_FOX, HEAD_DIM)
        return jnp.pad(w, ((0, 0), (0, 0), (0, K_WIDTH - HEAD_DIM))).reshape(D_MODEL, N_FOX * K_WIDTH)

    wn = jnp.concatenate([pad_heads(fk), pad_heads(mk)], axis=1).astype(BF16)
    wt = jnp.concatenate([fq.T, fv.T, mq.T, mv.T, ff.T,
                          jnp.zeros((LF_ROWS - N_FOX, D_MODEL), F32)], axis=0).astype(BF16)
    bf = jnp.broadcast_to(jnp.pad(b_f_l.astype(F32), (0, LF_ROWS - N_FOX))[:, None], (LF_ROWS, TILE))
    return wn, wt, bf


def kernel(x, w_in, b_f, w_o, ln1_g, ln1_b, w_gu, w_down, ln2_g, ln2_b):
    B, S, D = x.shape
    place = _placement_matrix()
    for l in range(DEPTH):
        wn, wt, bf = _layout_in_proj(w_in[l], b_f[l])
        k, qt, vt = _projection(x, wn, wt, bf, place)
        ot = _attention(k, qt, vt)
        x = _outproj_ln(ot, x, w_o[l].T.astype(BF16), ln1_g[l][None, :], ln1_b[l][None, :])
        x = _ffn_ln(x.reshape(B * S, D), w_gu[l].astype(BF16), w_down[l].astype(BF16),
                    ln2_g[l][None, :], ln2_b[l][None, :]).reshape(B, S, D)
    return x
```

```python
import jax
import jax.numpy as jnp
from jax import lax
from jax.experimental import pallas as pl
from jax.experimental.pallas import tpu as pltpu

D_MODEL = 1024
HEAD_DIM = 64
N_FOX = 8
N_MOBA = 8
N_HEADS = N_FOX + N_MOBA
GROUP_WIDTH = N_FOX * HEAD_DIM
MOBA_BLOCK = 256
MOBA_TOPK = 3
D_FF = 2816
DEPTH = 2
ALPHA = (2 * DEPTH) ** 0.25
LN_EPS = 1e-5
NEG_INF = -1e30

TILE = 256
Q_TILE = 2 * TILE
K_WIDTH = 128
AUG = K_WIDTH - HEAD_DIM
V_ROWS = 80
LF_ROWS = 16
VMEM_LIMIT = 56 * 1024 * 1024
HEADS_PER_STEP = 8
N_BLOCKS = 32
FFN_ROWS = 512
FFN_CHUNK = 256

F32 = jnp.float32
BF16 = jnp.bfloat16

ROW_LF = 0
ROW_QF, ROW_KF, ROW_VF, ROW_QM, ROW_KM, ROW_VM = (LF_ROWS + j * GROUP_WIDTH for j in range(6))

FOX_Q_C = 0
FOX_K_ONE = 0
MOBA_SEL = 0
MOBA_ALIBI = 32


def _dot(a, b):
    return jnp.dot(a, b, preferred_element_type=F32)


def _dot_nt(a, b):
    return lax.dot_general(a, b, (((1,), (1,)), ((), ())), preferred_element_type=F32)


def _split3(x):
    hi = x.astype(BF16).astype(F32)
    r = x - hi
    mid = r.astype(BF16).astype(F32)
    lo = (r - mid).astype(BF16).astype(F32)
    return hi, mid, lo


def _log_sigmoid(x):
    return jnp.minimum(x, 0.0) - jnp.log1p(jnp.exp(-jnp.abs(x)))


def _alibi_slope(h):
    return 2.0 ** (-8.0 * (h + 1) / N_MOBA)


def _proj_kernel(x_ref, wt_ref, bf_ref, k_ref, qt_ref, vt_ref, carry_ref, kmean_ref):
    i = pl.program_id(1)

    @pl.when(i == 0)
    def _():
        carry_ref[...] = jnp.zeros_like(carry_ref)
        kmean_ref[...] = jnp.zeros_like(kmean_ref)

    xb = x_ref[0].astype(BF16)
    tr = _dot_nt(wt_ref[...], xb)

    def head_rows(base, h):
        return tr[base + h * HEAD_DIM:base + (h + 1) * HEAD_DIM]

    row64 = lax.broadcasted_iota(jnp.int32, (AUG, TILE), 0)
    pos64 = lax.broadcasted_iota(jnp.int32, (AUG, TILE), 1).astype(F32)
    ones_rows = jnp.where(lax.broadcasted_iota(jnp.int32, (V_ROWS - HEAD_DIM, TILE), 0) == 0,
                          1.0, 0.0).astype(BF16)
    scale = HEAD_DIM ** -0.5

    def store_head(hh, q_t, q_aug, k_t, k_aug, v_t):
        qt_ref[0, hh, 0, 0:HEAD_DIM, :] = (q_t * scale).astype(BF16)
        qt_ref[0, hh, 0, HEAD_DIM:K_WIDTH, :] = q_aug.astype(BF16)
        vt_ref[0, hh, 0, 0:HEAD_DIM, :] = v_t.astype(BF16)
        vt_ref[0, hh, 0, HEAD_DIM:V_ROWS, :] = ones_rows
        k_nat = jnp.concatenate([k_t, k_aug], axis=0).T
        k_ref[0, hh, 0] = k_nat.astype(BF16)
        return k_nat

    lf = _log_sigmoid(tr[ROW_LF:ROW_LF + LF_ROWS] + bf_ref[...])
    tri = (lax.broadcasted_iota(jnp.int32, (TILE, TILE), 0)
           <= lax.broadcasted_iota(jnp.int32, (TILE, TILE), 1)).astype(BF16)
    ones_sq = jnp.ones((TILE, TILE), BF16)
    c_t = carry_ref[...]
    total = jnp.zeros_like(c_t)
    for part in _split3(lf):
        pb = part.astype(BF16)
        c_t = c_t + _dot(pb, tri)
        total = total + _dot(pb, ones_sq)
    carry_ref[...] = carry_ref[...] + total
    c_hi, c_mid, c_lo = _split3(c_t)

    for h in range(N_FOX):
        hi, mid, lo = c_hi[h:h + 1], c_mid[h:h + 1], c_lo[h:h + 1]
        q_aug = jnp.where(row64 == FOX_Q_C, hi,
                jnp.where(row64 == FOX_Q_C + 1, mid,
                jnp.where(row64 == FOX_Q_C + 2, lo,
                jnp.where(row64 < FOX_Q_C + 6, 1.0, 0.0))))
        k_aug = jnp.where(row64 < FOX_K_ONE + 3, 1.0,
                jnp.where(row64 == FOX_K_ONE + 3, -hi,
                jnp.where(row64 == FOX_K_ONE + 4, -mid,
                jnp.where(row64 == FOX_K_ONE + 5, -lo, 0.0))))
        store_head(h, head_rows(ROW_QF, h), q_aug, head_rows(ROW_KF, h), k_aug, head_rows(ROW_VF, h))

    blk = lax.broadcasted_iota(jnp.int32, (N_BLOCKS, TILE), 0)
    past = blk < i
    base_pos = MOBA_BLOCK * i.astype(F32)
    for h in range(N_MOBA):
        hh = N_FOX + h
        slope = _alibi_slope(h)
        q_t = head_rows(ROW_QM, h)
        km = kmean_ref[h, :, 0:HEAD_DIM]
        km_hi, km_mid, _ = _split3(km)
        q_hi, q_mid, _ = _split3(q_t)
        g = (_dot(km_hi.astype(BF16), q_hi.astype(BF16))
             + _dot(km_hi.astype(BF16), q_mid.astype(BF16))
             + _dot(km_mid.astype(BF16), q_hi.astype(BF16)))
        remaining = past
        for _ in range(MOBA_TOPK):
            gm = jnp.where(remaining, g, -jnp.inf)
            mx = jnp.max(gm, axis=0, keepdims=True)
            cand = remaining & (gm == mx)
            first = jnp.min(jnp.where(cand, blk, N_BLOCKS), axis=0, keepdims=True)
            remaining = remaining & (blk != first)
        selected = (past & jnp.logical_not(remaining)) | (blk == i)
        r = row64 - MOBA_ALIBI
        q_aug = jnp.where(r == 0, -slope * base_pos,
                jnp.where(r == 1, -slope * pos64,
                jnp.where((r == 2) | (r == 3), 1.0, 0.0)))
        q_aug = jnp.concatenate([jnp.where(selected, 0.0, NEG_INF), q_aug[MOBA_ALIBI:]], axis=0)
        k_aug = jnp.where(row64 == MOBA_SEL + i, 1.0,
                jnp.where((r == 0) | (r == 1), 1.0,
                jnp.where(r == 2, slope * base_pos,
                jnp.where(r == 3, slope * pos64, 0.0))))
        k_nat = store_head(hh, q_t, q_aug, head_rows(ROW_KM, h), k_aug, head_rows(ROW_VM, h))
        kmean_ref[h, pl.ds(i, 1), :] = jnp.sum(k_nat, axis=0, keepdims=True) * (1.0 / MOBA_BLOCK)


def _projection(x, wt, bf):
    B, S, _ = x.shape
    n_t = S // TILE
    q_per = Q_TILE // TILE
    const = lambda b, i: (0, 0)
    return pl.pallas_call(
        _proj_kernel,
        grid=(B, n_t),
        in_specs=[
            pl.BlockSpec((1, TILE, D_MODEL), lambda b, i: (b, i, 0)),
            pl.BlockSpec(wt.shape, const, pipeline_mode=pl.Buffered(1)),
            pl.BlockSpec(bf.shape, const, pipeline_mode=pl.Buffered(1)),
        ],
        out_specs=[
            pl.BlockSpec((1, N_HEADS, 1, TILE, K_WIDTH), lambda b, i: (b, 0, i, 0, 0)),
            pl.BlockSpec((1, N_HEADS, 1, K_WIDTH, TILE), lambda b, i: (b, 0, i // q_per, 0, i % q_per)),
            pl.BlockSpec((1, N_HEADS, 1, V_ROWS, TILE), lambda b, i: (b, 0, i, 0, 0)),
        ],
        out_shape=[
            jax.ShapeDtypeStruct((B, N_HEADS, n_t, TILE, K_WIDTH), BF16),
            jax.ShapeDtypeStruct((B, N_HEADS, S // Q_TILE, K_WIDTH, Q_TILE), BF16),
            jax.ShapeDtypeStruct((B, N_HEADS, n_t, V_ROWS, TILE), BF16),
        ],
        scratch_shapes=[
            pltpu.VMEM((LF_ROWS, TILE), F32),
            pltpu.VMEM((N_MOBA, N_BLOCKS, K_WIDTH), F32),
        ],
        compiler_params=pltpu.CompilerParams(
            dimension_semantics=("arbitrary", "arbitrary"), vmem_limit_bytes=VMEM_LIMIT),
        name="projection",
    )(x, wt, bf)


def _attn_kernel(k_ref, qt_ref, vt_ref, o_ref, s_ref, tmax_ref, m_ref, acc_ref):
    qi = pl.program_id(2)
    n_h = qt_ref.shape[1]
    tiles_per_q = Q_TILE // TILE
    first_masked = tiles_per_q * qi
    m_ref[...] = jnp.full(m_ref.shape, -jnp.inf, F32)
    acc_ref[...] = jnp.zeros(acc_ref.shape, F32)

    def logits(h, kj, slot, masked):
        s = _dot(k_ref[0, h, kj], qt_ref[0, h, 0])
        if masked:
            key_idx = lax.broadcasted_iota(jnp.int32, (TILE, Q_TILE), 0)
            qry_idx = lax.broadcasted_iota(jnp.int32, (TILE, Q_TILE), 1)
            s = jnp.where(key_idx + (kj - first_masked) * TILE <= qry_idx, s, NEG_INF)
        s_ref[slot, h] = s
        tmax_ref[slot, h] = jnp.max(s, axis=0, keepdims=True)

    def accumulate(h, kj, slot):
        m = m_ref[h]
        m_new = jnp.maximum(m, tmax_ref[slot, h])
        p = jnp.exp(s_ref[slot, h] - m_new).astype(BF16)
        acc_ref[h] = acc_ref[h] * jnp.exp(m - m_new) + _dot(vt_ref[0, h, kj], p)
        m_ref[h] = m_new

    def pair(kj, slot, masked):
        for h in range(n_h):
            logits(h, kj + 1, 1 - slot, masked)
            accumulate(h, kj, slot)

    for h in range(n_h):
        logits(h, 0, 0, True)

    @pl.when(qi > 0)
    def _():
        pair(0, 0, False)

        def two_pairs(i, c):
            pair(2 * i + 1, 1, False)
            pair(2 * i + 2, 0, False)
            return c
        lax.fori_loop(0, qi - 1, two_pairs, 0)
        pair(first_masked - 1, 1, True)

    pair(first_masked, 0, True)
    for h in range(n_h):
        accumulate(h, first_masked + 1, 1)
    for h in range(n_h):
        acc = acc_ref[h]
        o_ref[0, h, 0] = (acc[0:HEAD_DIM] / acc[HEAD_DIM:HEAD_DIM + 1]).astype(BF16)


def _attention(k, qt, vt):
    B, H, n_t = k.shape[:3]
    n_q = qt.shape[2]
    hps = HEADS_PER_STEP
    resident = lambda r, c: pl.BlockSpec((1, hps, n_t, r, c), lambda b, g, i: (b, g, 0, 0, 0),
                                         pipeline_mode=pl.Buffered(1))
    tiled = lambda r, c: pl.BlockSpec((1, hps, 1, r, c), lambda b, g, i: (b, g, i, 0, 0))
    return pl.pallas_call(
        _attn_kernel,
        grid=(B, H // hps, n_q),
        in_specs=[resident(TILE, K_WIDTH), tiled(K_WIDTH, Q_TILE), resident(V_ROWS, TILE)],
        out_specs=tiled(HEAD_DIM, Q_TILE),
        out_shape=jax.ShapeDtypeStruct((B, H, n_q, HEAD_DIM, Q_TILE), BF16),
        scratch_shapes=[
            pltpu.VMEM((2, hps, TILE, Q_TILE), F32),
            pltpu.VMEM((2, hps, 1, Q_TILE), F32),
            pltpu.VMEM((hps, 1, Q_TILE), F32),
            pltpu.VMEM((hps, V_ROWS, Q_TILE), F32),
        ],
        compiler_params=pltpu.CompilerParams(
            dimension_semantics=("arbitrary", "arbitrary", "arbitrary"), vmem_limit_bytes=VMEM_LIMIT),
        name="attention",
    )(k, qt, vt)


def _layer_norm(z, g, b):
    mu = jnp.mean(z, axis=-1, keepdims=True)
    d = z - mu
    var = jnp.mean(d * d, axis=-1, keepdims=True)
    return d * lax.rsqrt(var + LN_EPS) * g + b


def _outproj_kernel(ot_ref, x_ref, wo_ref, g_ref, b_ref, out_ref):
    heads_t = ot_ref[0, :, 0].reshape(N_HEADS * HEAD_DIM, TILE)
    y = _dot(wo_ref[...], heads_t).T
    out_ref[0] = _layer_norm(ALPHA * x_ref[0] + y, g_ref[...], b_ref[...])


def _outproj_ln(ot, x, wo_t, g, b):
    B, S, _ = x.shape
    n_t = S // TILE
    q_per = Q_TILE // TILE
    const = lambda bb, i: (0, 0)
    return pl.pallas_call(
        _outproj_kernel,
        grid=(B, n_t),
        in_specs=[
            pl.BlockSpec((1, N_HEADS, 1, HEAD_DIM, TILE), lambda bb, i: (bb, 0, i // q_per, 0, i % q_per)),
            pl.BlockSpec((1, TILE, D_MODEL), lambda bb, i: (bb, i, 0)),
            pl.BlockSpec(wo_t.shape, const, pipeline_mode=pl.Buffered(1)),
            pl.BlockSpec(g.shape, const),
            pl.BlockSpec(b.shape, const),
        ],
        out_specs=pl.BlockSpec((1, TILE, D_MODEL), lambda bb, i: (bb, i, 0)),
        out_shape=jax.ShapeDtypeStruct(x.shape, F32),
        compiler_params=pltpu.CompilerParams(
            dimension_semantics=("arbitrary", "arbitrary"), vmem_limit_bytes=VMEM_LIMIT),
        name="outproj_ln",
    )(ot, x, wo_t, g, b)


def _ffn_kernel(x_ref, wgu_ref, wd_ref, g_ref, b_ref, out_ref):
    x = x_ref[...]
    xb = x.astype(BF16)

    def gate_up(c):
        cols = slice(c * FFN_CHUNK, (c + 1) * FFN_CHUNK)
        cols_up = slice(D_FF + c * FFN_CHUNK, D_FF + (c + 1) * FFN_CHUNK)
        return _dot(xb, wgu_ref[:, cols]), _dot(xb, wgu_ref[:, cols_up])

    n_chunks = D_FF // FFN_CHUNK
    y = jnp.zeros(x.shape, F32)
    nxt = gate_up(0)
    for c in range(n_chunks):
        gate, up = nxt
        if c + 1 < n_chunks:
            nxt = gate_up(c + 1)
        hidden = (gate * jax.nn.sigmoid(gate) * up).astype(BF16)
        y = y + _dot(hidden, wd_ref[c * FFN_CHUNK:(c + 1) * FFN_CHUNK, :])
    out_ref[...] = _layer_norm(ALPHA * x + y, g_ref[...], b_ref[...])


def _ffn_ln(x2d, wgu, wd, g, b):
    n = x2d.shape[0]
    const = lambda i: (0, 0)
    return pl.pallas_call(
        _ffn_kernel,
        grid=(n // FFN_ROWS,),
        in_specs=[
            pl.BlockSpec((FFN_ROWS, D_MODEL), lambda i: (i, 0)),
            pl.BlockSpec(wgu.shape, const, pipeline_mode=pl.Buffered(1)),
            pl.BlockSpec(wd.shape, const, pipeline_mode=pl.Buffered(1)),
            pl.BlockSpec(g.shape, const),
            pl.BlockSpec(b.shape, const),
        ],
        out_specs=pl.BlockSpec((FFN_ROWS, D_MODEL), lambda i: (i, 0)),
        out_shape=jax.ShapeDtypeStruct(x2d.shape, F32),
        compiler_params=pltpu.CompilerParams(
            dimension_semantics=("arbitrary",), vmem_limit_bytes=VMEM_LIMIT),
        name="ffn_ln",
    )(x2d, wgu, wd, g, b)


def _layout_in_proj(w_in_l, b_f_l):
    fq, fk, fv, ff, mq, mk, mv = jnp.split(
        w_in_l, [GROUP_WIDTH, 2 * GROUP_WIDTH, 3 * GROUP_WIDTH, 3 * GROUP_WIDTH + N_FOX,
                 4 * GROUP_WIDTH + N_FOX, 5 * GROUP_WIDTH + N_FOX], axis=1)
    wt = jnp.concatenate([ff.T, jnp.zeros((LF_ROWS - N_FOX, D_MODEL), F32),
                          fq.T, fk.T, fv.T, mq.T, mk.T, mv.T], axis=0).astype(BF16)
    bf = jnp.broadcast_to(jnp.pad(b_f_l.astype(F32), (0, LF_ROWS - N_FOX))[:, None], (LF_ROWS, TILE))
    return wt, bf


def kernel(x, w_in, b_f, w_o, ln1_g, ln1_b, w_gu, w_down, ln2_g, ln2_b):
    B, S, D = x.shape
    for l in range(DEPTH):
        wt, bf = _layout_in_proj(w_in[l], b_f[l])
        k, qt, vt = _projection(x, wt, bf)
        ot = _attention(k, qt, vt)
        x = _outproj_ln(ot, x, w_o[l].T.astype(BF16), ln1_g[l][None, :], ln1_b[l][None, :])
        x = _ffn_ln(x.reshape(B * S, D), w_gu[l].astype(BF16), w_down[l].astype(BF16),
                    ln2_g[l][None, :], ln2_b[l][None, :]).reshape(B, S, D)
    return x
```

```python
import jax
import jax.numpy as jnp
from jax import lax
from jax.experimental import pallas as pl
from jax.experimental.pallas import tpu as pltpu

D_MODEL = 1024
HEAD_DIM = 64
N_FOX = 8
N_MOBA = 8
N_HEADS = N_FOX + N_MOBA
GROUP_WIDTH = N_FOX * HEAD_DIM
MOBA_BLOCK = 256
MOBA_TOPK = 3
D_FF = 2816
DEPTH = 2
ALPHA = (2 * DEPTH) ** 0.25
LN_EPS = 1e-5
NEG_INF = -1e30
LOG2E = 1.4426950408889634

TILE = 256
Q_TILE = 2 * TILE
K_WIDTH = 128
AUG = K_WIDTH - HEAD_DIM
V_ROWS = 80
LF_ROWS = 16
VMEM_LIMIT = 56 * 1024 * 1024
HEADS_PER_STEP = 8
N_BLOCKS = 32
FFN_ROWS = 512
FFN_CHUNK = 256

F32 = jnp.float32
BF16 = jnp.bfloat16

ROW_LF = 0
ROW_QF, ROW_KF, ROW_VF, ROW_QM, ROW_KM, ROW_VM = (LF_ROWS + j * GROUP_WIDTH for j in range(6))

FOX_Q_C = 0
FOX_K_ONE = 0
MOBA_SEL = 0
MOBA_ALIBI = 32


def _dot(a, b):
    return jnp.dot(a, b, preferred_element_type=F32)


def _dot_nt(a, b):
    return lax.dot_general(a, b, (((1,), (1,)), ((), ())), preferred_element_type=F32)


def _split3(x):
    hi = x.astype(BF16).astype(F32)
    r = x - hi
    mid = r.astype(BF16).astype(F32)
    lo = (r - mid).astype(BF16).astype(F32)
    return hi, mid, lo


def _log_sigmoid(x):
    return jnp.minimum(x, 0.0) - jnp.log1p(jnp.exp(-jnp.abs(x)))


def _alibi_slope(h):
    return 2.0 ** (-8.0 * (h + 1) / N_MOBA)


def _proj_kernel(x_ref, wt_ref, bf_ref, k_ref, qt_ref, vt_ref, carry_ref, kmean_ref):
    i = pl.program_id(1)

    @pl.when(i == 0)
    def _():
        carry_ref[...] = jnp.zeros_like(carry_ref)
        kmean_ref[...] = jnp.zeros_like(kmean_ref)

    xb = x_ref[0].astype(BF16)
    tr = _dot_nt(wt_ref[...], xb)

    def head_rows(base, h):
        return tr[base + h * HEAD_DIM:base + (h + 1) * HEAD_DIM]

    row64 = lax.broadcasted_iota(jnp.int32, (AUG, TILE), 0)
    pos64 = lax.broadcasted_iota(jnp.int32, (AUG, TILE), 1).astype(F32)
    ones_rows = jnp.where(lax.broadcasted_iota(jnp.int32, (V_ROWS - HEAD_DIM, TILE), 0) == 0,
                          1.0, 0.0).astype(BF16)
    scale = HEAD_DIM ** -0.5 * LOG2E

    def store_head(hh, q_t, q_aug, k_t, k_aug, v_t):
        qt_ref[0, hh, 0, 0:HEAD_DIM, :] = (q_t * scale).astype(BF16)
        qt_ref[0, hh, 0, HEAD_DIM:K_WIDTH, :] = q_aug.astype(BF16)
        vt_ref[0, hh, 0, 0:HEAD_DIM, :] = v_t.astype(BF16)
        vt_ref[0, hh, 0, HEAD_DIM:V_ROWS, :] = ones_rows
        k_nat = jnp.concatenate([k_t, k_aug], axis=0).T
        k_ref[0, hh, 0] = k_nat.astype(BF16)
        return k_nat

    lf = _log_sigmoid(tr[ROW_LF:ROW_LF + LF_ROWS] + bf_ref[...])
    tri = (lax.broadcasted_iota(jnp.int32, (TILE, TILE), 0)
           <= lax.broadcasted_iota(jnp.int32, (TILE, TILE), 1)).astype(BF16)
    ones_sq = jnp.ones((TILE, TILE), BF16)
    c_t = carry_ref[...]
    total = jnp.zeros_like(c_t)
    for part in _split3(lf):
        pb = part.astype(BF16)
        c_t = c_t + _dot(pb, tri)
        total = total + _dot(pb, ones_sq)
    carry_ref[...] = carry_ref[...] + total
    c_hi, c_mid, c_lo = _split3(c_t * LOG2E)

    for h in range(N_FOX):
        hi, mid, lo = c_hi[h:h + 1], c_mid[h:h + 1], c_lo[h:h + 1]
        q_aug = jnp.where(row64 == FOX_Q_C, hi,
                jnp.where(row64 == FOX_Q_C + 1, mid,
                jnp.where(row64 == FOX_Q_C + 2, lo,
                jnp.where(row64 < FOX_Q_C + 6, 1.0, 0.0))))
        k_aug = jnp.where(row64 < FOX_K_ONE + 3, 1.0,
                jnp.where(row64 == FOX_K_ONE + 3, -hi,
                jnp.where(row64 == FOX_K_ONE + 4, -mid,
                jnp.where(row64 == FOX_K_ONE + 5, -lo, 0.0))))
        store_head(h, head_rows(ROW_QF, h), q_aug, head_rows(ROW_KF, h), k_aug, head_rows(ROW_VF, h))

    blk = lax.broadcasted_iota(jnp.int32, (N_BLOCKS, TILE), 0)
    past = blk < i
    base_pos = MOBA_BLOCK * i.astype(F32)
    for h in range(N_MOBA):
        hh = N_FOX + h
        slope = _alibi_slope(h)
        q_t = head_rows(ROW_QM, h)
        km = kmean_ref[h, :, 0:HEAD_DIM]
        km_hi, km_mid, _ = _split3(km)
        q_hi, q_mid, _ = _split3(q_t)
        g = (_dot(km_hi.astype(BF16), q_hi.astype(BF16))
             + _dot(km_hi.astype(BF16), q_mid.astype(BF16))
             + _dot(km_mid.astype(BF16), q_hi.astype(BF16)))
        remaining = past
        for _ in range(MOBA_TOPK):
            gm = jnp.where(remaining, g, -jnp.inf)
            mx = jnp.max(gm, axis=0, keepdims=True)
            cand = remaining & (gm == mx)
            first = jnp.min(jnp.where(cand, blk, N_BLOCKS), axis=0, keepdims=True)
            remaining = remaining & (blk != first)
        selected = (past & jnp.logical_not(remaining)) | (blk == i)
        r = row64 - MOBA_ALIBI
        d_hi, d_mid, d_lo = _split3((slope * LOG2E) * (base_pos + pos64))
        q_aug = jnp.where(r == 0, -d_hi,
                jnp.where(r == 1, -d_mid,
                jnp.where(r == 2, -d_lo,
                jnp.where((r >= 3) & (r < 6), 1.0, 0.0))))
        q_aug = jnp.concatenate([jnp.where(selected, 0.0, NEG_INF), q_aug[MOBA_ALIBI:]], axis=0)
        k_aug = jnp.where(row64 == MOBA_SEL + i, 1.0,
                jnp.where((r >= 0) & (r < 3), 1.0,
                jnp.where(r == 3, d_hi,
                jnp.where(r == 4, d_mid,
                jnp.where(r == 5, d_lo, 0.0)))))
        k_nat = store_head(hh, q_t, q_aug, head_rows(ROW_KM, h), k_aug, head_rows(ROW_VM, h))
        kmean_ref[h, pl.ds(i, 1), :] = jnp.sum(k_nat, axis=0, keepdims=True) * (1.0 / MOBA_BLOCK)


def _projection(x, wt, bf):
    B, S, _ = x.shape
    n_t = S // TILE
    q_per = Q_TILE // TILE
    const = lambda b, i: (0, 0)
    return pl.pallas_call(
        _proj_kernel,
        grid=(B, n_t),
        in_specs=[
            pl.BlockSpec((1, TILE, D_MODEL), lambda b, i: (b, i, 0)),
            pl.BlockSpec(wt.shape, const, pipeline_mode=pl.Buffered(1)),
            pl.BlockSpec(bf.shape, const, pipeline_mode=pl.Buffered(1)),
        ],
        out_specs=[
            pl.BlockSpec((1, N_HEADS, 1, TILE, K_WIDTH), lambda b, i: (b, 0, i, 0, 0)),
            pl.BlockSpec((1, N_HEADS, 1, K_WIDTH, TILE), lambda b, i: (b, 0, i // q_per, 0, i % q_per)),
            pl.BlockSpec((1, N_HEADS, 1, V_ROWS, TILE), lambda b, i: (b, 0, i, 0, 0)),
        ],
        out_shape=[
            jax.ShapeDtypeStruct((B, N_HEADS, n_t, TILE, K_WIDTH), BF16),
            jax.ShapeDtypeStruct((B, N_HEADS, S // Q_TILE, K_WIDTH, Q_TILE), BF16),
            jax.ShapeDtypeStruct((B, N_HEADS, n_t, V_ROWS, TILE), BF16),
        ],
        scratch_shapes=[
            pltpu.VMEM((LF_ROWS, TILE), F32),
            pltpu.VMEM((N_MOBA, N_BLOCKS, K_WIDTH), F32),
        ],
        compiler_params=pltpu.CompilerParams(
            dimension_semantics=("arbitrary", "arbitrary"), vmem_limit_bytes=VMEM_LIMIT),
        name="projection",
    )(x, wt, bf)


def _attn_kernel(k_ref, qt_ref, vt_ref, o_ref, s_ref, tmax_ref, m_ref, acc_ref):
    qi = pl.program_id(2)
    n_h = qt_ref.shape[1]
    tiles_per_q = Q_TILE // TILE
    first_masked = tiles_per_q * qi
    m_ref[...] = jnp.full(m_ref.shape, -jnp.inf, F32)
    acc_ref[...] = jnp.zeros(acc_ref.shape, F32)

    def logits(h, kj, slot, masked):
        s = _dot(k_ref[0, h, kj], qt_ref[0, h, 0])
        if masked:
            key_idx = lax.broadcasted_iota(jnp.int32, (TILE, Q_TILE), 0)
            qry_idx = lax.broadcasted_iota(jnp.int32, (TILE, Q_TILE), 1)
            s = jnp.where(key_idx + (kj - first_masked) * TILE <= qry_idx, s, NEG_INF)
        s_ref[slot, h] = s
        tmax_ref[slot, h] = jnp.max(s, axis=0, keepdims=True)

    def accumulate(h, kj, slot):
        m = m_ref[h]
        m_new = jnp.maximum(m, tmax_ref[slot, h])
        p = jnp.exp2(s_ref[slot, h] - m_new).astype(BF16)
        acc_ref[h] = acc_ref[h] * jnp.exp2(m - m_new) + _dot(vt_ref[0, h, kj], p)
        m_ref[h] = m_new

    def pair(kj, slot, masked):
        for h in range(n_h):
            logits(h, kj + 1, 1 - slot, masked)
            accumulate(h, kj, slot)

    for h in range(n_h):
        logits(h, 0, 0, True)

    @pl.when(qi > 0)
    def _():
        pair(0, 0, False)

        def two_pairs(i, c):
            pair(2 * i + 1, 1, False)
            pair(2 * i + 2, 0, False)
            return c
        lax.fori_loop(0, qi - 1, two_pairs, 0)
        pair(first_masked - 1, 1, True)

    pair(first_masked, 0, True)
    for h in range(n_h):
        accumulate(h, first_masked + 1, 1)
    for h in range(n_h):
        acc = acc_ref[h]
        o_ref[0, h, 0] = (acc[0:HEAD_DIM] / acc[HEAD_DIM:HEAD_DIM + 1]).astype(BF16)


def _attention(k, qt, vt):
    B, H, n_t = k.shape[:3]
    n_q = qt.shape[2]
    hps = HEADS_PER_STEP
    resident = lambda r, c, bufs: pl.BlockSpec((1, hps, n_t, r, c), lambda b, g, i: (b, g, 0, 0, 0),
                                               pipeline_mode=pl.Buffered(bufs))
    tiled = lambda r, c: pl.BlockSpec((1, hps, 1, r, c), lambda b, g, i: (b, g, i, 0, 0))
    return pl.pallas_call(
        _attn_kernel,
        grid=(B, H // hps, n_q),
        in_specs=[resident(TILE, K_WIDTH, 1), tiled(K_WIDTH, Q_TILE), resident(V_ROWS, TILE, 2)],
        out_specs=tiled(HEAD_DIM, Q_TILE),
        out_shape=jax.ShapeDtypeStruct((B, H, n_q, HEAD_DIM, Q_TILE), BF16),
        scratch_shapes=[
            pltpu.VMEM((2, hps, TILE, Q_TILE), F32),
            pltpu.VMEM((2, hps, 1, Q_TILE), F32),
            pltpu.VMEM((hps, 1, Q_TILE), F32),
            pltpu.VMEM((hps, V_ROWS, Q_TILE), F32),
        ],
        compiler_params=pltpu.CompilerParams(
            dimension_semantics=("arbitrary", "arbitrary", "arbitrary"), vmem_limit_bytes=VMEM_LIMIT),
        name="attention",
    )(k, qt, vt)


def _layer_norm(z, g, b):
    mu = jnp.mean(z, axis=-1, keepdims=True)
    d = z - mu
    var = jnp.mean(d * d, axis=-1, keepdims=True)
    return d * lax.rsqrt(var + LN_EPS) * g + b


def _outproj_kernel(ot_ref, x_ref, wo_ref, g_ref, b_ref, out_ref):
    halves = []
    for j in range(Q_TILE // TILE):
        heads_t = ot_ref[0, :, 0, :, j * TILE:(j + 1) * TILE].reshape(N_HEADS * HEAD_DIM, TILE)
        halves.append(lax.dot_general(heads_t, wo_ref[...], (((0,), (0,)), ((), ())),
                                      preferred_element_type=F32))
    for j, y in enumerate(halves):
        rows = slice(j * TILE, (j + 1) * TILE)
        out_ref[0, rows] = _layer_norm(ALPHA * x_ref[0, rows] + y, g_ref[...], b_ref[...])


def _outproj_ln(ot, x, wo, g, b):
    B, S, _ = x.shape
    const = lambda bb, i: (0, 0)
    return pl.pallas_call(
        _outproj_kernel,
        grid=(B, S // Q_TILE),
        in_specs=[
            pl.BlockSpec((1, N_HEADS, 1, HEAD_DIM, Q_TILE), lambda bb, i: (bb, 0, i, 0, 0)),
            pl.BlockSpec((1, Q_TILE, D_MODEL), lambda bb, i: (bb, i, 0)),
            pl.BlockSpec(wo.shape, const, pipeline_mode=pl.Buffered(1)),
            pl.BlockSpec(g.shape, const),
            pl.BlockSpec(b.shape, const),
        ],
        out_specs=pl.BlockSpec((1, Q_TILE, D_MODEL), lambda bb, i: (bb, i, 0)),
        out_shape=jax.ShapeDtypeStruct(x.shape, F32),
        compiler_params=pltpu.CompilerParams(
            dimension_semantics=("arbitrary", "arbitrary"), vmem_limit_bytes=VMEM_LIMIT),
        name="outproj_ln",
    )(ot, x, wo, g, b)


def _ffn_kernel(x_ref, wgu_ref, wd_ref, g_ref, b_ref, out_ref):
    x = x_ref[...]
    xb = x.astype(BF16)

    def gate_up(c):
        cols = slice(c * FFN_CHUNK, (c + 1) * FFN_CHUNK)
        cols_up = slice(D_FF + c * FFN_CHUNK, D_FF + (c + 1) * FFN_CHUNK)
        return _dot(xb, wgu_ref[:, cols]), _dot(xb, wgu_ref[:, cols_up])

    n_chunks = D_FF // FFN_CHUNK
    y = jnp.zeros(x.shape, F32)
    nxt = gate_up(0)
    for c in range(n_chunks):
        gate, up = nxt
        if c + 1 < n_chunks:
            nxt = gate_up(c + 1)
        hidden = (gate * jax.nn.sigmoid(gate) * up).astype(BF16)
        y = y + _dot(hidden, wd_ref[c * FFN_CHUNK:(c + 1) * FFN_CHUNK, :])
    out_ref[...] = _layer_norm(ALPHA * x + y, g_ref[...], b_ref[...])


def _ffn_ln(x2d, wgu, wd, g, b):
    n = x2d.shape[0]
    const = lambda i: (0, 0)
    return pl.pallas_call(
        _ffn_kernel,
        grid=(n // FFN_ROWS,),
        in_specs=[
            pl.BlockSpec((FFN_ROWS, D_MODEL), lambda i: (i, 0)),
            pl.BlockSpec(wgu.shape, const, pipeline_mode=pl.Buffered(1)),
            pl.BlockSpec(wd.shape, const, pipeline_mode=pl.Buffered(1)),
            pl.BlockSpec(g.shape, const),
            pl.BlockSpec(b.shape, const),
        ],
        out_specs=pl.BlockSpec((FFN_ROWS, D_MODEL), lambda i: (i, 0)),
        out_shape=jax.ShapeDtypeStruct(x2d.shape, F32),
        compiler_params=pltpu.CompilerParams(
            dimension_semantics=("arbitrary",), vmem_limit_bytes=VMEM_LIMIT),
        name="ffn_ln",
    )(x2d, wgu, wd, g, b)


def _layout_in_proj(w_in_l, b_f_l):
    fq, fk, fv, ff, mq, mk, mv = jnp.split(
        w_in_l, [GROUP_WIDTH, 2 * GROUP_WIDTH, 3 * GROUP_WIDTH, 3 * GROUP_WIDTH + N_FOX,
                 4 * GROUP_WIDTH + N_FOX, 5 * GROUP_WIDTH + N_FOX], axis=1)
    wt = jnp.concatenate([ff.T, jnp.zeros((LF_ROWS - N_FOX, D_MODEL), F32),
                          fq.T, fk.T, fv.T, mq.T, mk.T, mv.T], axis=0).astype(BF16)
    bf = jnp.broadcast_to(jnp.pad(b_f_l.astype(F32), (0, LF_ROWS - N_FOX))[:, None], (LF_ROWS, TILE))
    return wt, bf


def kernel(x, w_in, b_f, w_o, ln1_g, ln1_b, w_gu, w_down, ln2_g, ln2_b):
    B, S, D = x.shape
    for l in range(DEPTH):
        wt, bf = _layout_in_proj(w_in[l], b_f[l])
        k, qt, vt = _projection(x, wt, bf)
        ot = _attention(k, qt, vt)
        x = _outproj_ln(ot, x, w_o[l].astype(BF16), ln1_g[l][None, :], ln1_b[l][None, :])
        x = _ffn_ln(x.reshape(B * S, D), w_gu[l].astype(BF16), w_down[l].astype(BF16),
                    ln2_g[l][None, :], ln2_b[l][None, :]).reshape(B, S, D)
    return x
```

```python
import jax
import jax.numpy as jnp
from jax import lax
from jax.experimental import pallas as pl
from jax.experimental.pallas import tpu as pltpu

D_MODEL = 1024
HEAD_DIM = 64
N_FOX = 8
N_MOBA = 8
N_HEADS = N_FOX + N_MOBA
GROUP_WIDTH = N_FOX * HEAD_DIM
MOBA_BLOCK = 256
MOBA_TOPK = 3
D_FF = 2816
DEPTH = 2
ALPHA = (2 * DEPTH) ** 0.25
LN_EPS = 1e-5
NEG_INF = -1e30
LOG2E = 1.4426950408889634

TILE = 256
Q_TILE = 2 * TILE
K_WIDTH = 128
AUG = K_WIDTH - HEAD_DIM
V_ROWS = 80
LF_ROWS = 16
VMEM_LIMIT = 56 * 1024 * 1024
HEADS_PER_STEP = 8
N_BLOCKS = 32
FFN_ROWS = 512
FFN_CHUNK = 256

F32 = jnp.float32
BF16 = jnp.bfloat16

ROW_LF = 0
ROW_QF, ROW_KF, ROW_VF, ROW_QM, ROW_KM, ROW_VM = (LF_ROWS + j * GROUP_WIDTH for j in range(6))

FOX_Q_C = 0
FOX_K_ONE = 0
MOBA_SEL = 0
MOBA_ALIBI = 32


def _dot(a, b):
    return jnp.dot(a, b, preferred_element_type=F32)


def _dot_nt(a, b):
    return lax.dot_general(a, b, (((1,), (1,)), ((), ())), preferred_element_type=F32)


def _split3(x):
    hi = x.astype(BF16).astype(F32)
    r = x - hi
    mid = r.astype(BF16).astype(F32)
    lo = (r - mid).astype(BF16).astype(F32)
    return hi, mid, lo


def _log_sigmoid(x):
    return jnp.minimum(x, 0.0) - jnp.log1p(jnp.exp(-jnp.abs(x)))


def _alibi_slope(h):
    return 2.0 ** (-8.0 * (h + 1) / N_MOBA)


def _proj_kernel(x_ref, wt_ref, bf_ref, k_ref, qt_ref, vt_ref, carry_ref, kmean_ref):
    i = pl.program_id(1)

    @pl.when(i == 0)
    def _():
        carry_ref[...] = jnp.zeros_like(carry_ref)
        kmean_ref[...] = jnp.zeros_like(kmean_ref)

    xb = x_ref[0].astype(BF16)
    tr = _dot_nt(wt_ref[...], xb)

    def head_rows(base, h):
        return tr[base + h * HEAD_DIM:base + (h + 1) * HEAD_DIM]

    row64 = lax.broadcasted_iota(jnp.int32, (AUG, TILE), 0)
    pos64 = lax.broadcasted_iota(jnp.int32, (AUG, TILE), 1).astype(F32)
    ones_rows = jnp.where(lax.broadcasted_iota(jnp.int32, (V_ROWS - HEAD_DIM, TILE), 0) == 0,
                          1.0, 0.0).astype(BF16)
    scale = HEAD_DIM ** -0.5 * LOG2E

    def store_head(hh, q_t, q_aug, k_t, k_aug, v_t):
        qt_ref[0, hh, 0, 0:HEAD_DIM, :] = (q_t * scale).astype(BF16)
        qt_ref[0, hh, 0, HEAD_DIM:K_WIDTH, :] = q_aug.astype(BF16)
        vt_ref[0, hh, 0, 0:HEAD_DIM, :] = v_t.astype(BF16)
        vt_ref[0, hh, 0, HEAD_DIM:V_ROWS, :] = ones_rows
        k_nat = jnp.concatenate([k_t, k_aug], axis=0).T
        k_ref[0, hh, 0] = k_nat.astype(BF16)
        return k_nat

    lf = _log_sigmoid(tr[ROW_LF:ROW_LF + LF_ROWS] + bf_ref[...])
    tri = (lax.broadcasted_iota(jnp.int32, (TILE, TILE), 0)
           <= lax.broadcasted_iota(jnp.int32, (TILE, TILE), 1)).astype(BF16)
    ones_sq = jnp.ones((TILE, TILE), BF16)
    c_t = carry_ref[...]
    total = jnp.zeros_like(c_t)
    for part in _split3(lf):
        pb = part.astype(BF16)
        c_t = c_t + _dot(pb, tri)
        total = total + _dot(pb, ones_sq)
    carry_ref[...] = carry_ref[...] + total
    c_hi, c_mid, c_lo = _split3(c_t * LOG2E)

    for h in range(N_FOX):
        hi, mid, lo = c_hi[h:h + 1], c_mid[h:h + 1], c_lo[h:h + 1]
        q_aug = jnp.where(row64 == FOX_Q_C, hi,
                jnp.where(row64 == FOX_Q_C + 1, mid,
                jnp.where(row64 == FOX_Q_C + 2, lo,
                jnp.where(row64 < FOX_Q_C + 6, 1.0, 0.0))))
        k_aug = jnp.where(row64 < FOX_K_ONE + 3, 1.0,
                jnp.where(row64 == FOX_K_ONE + 3, -hi,
                jnp.where(row64 == FOX_K_ONE + 4, -mid,
                jnp.where(row64 == FOX_K_ONE + 5, -lo, 0.0))))
        store_head(h, head_rows(ROW_QF, h), q_aug, head_rows(ROW_KF, h), k_aug, head_rows(ROW_VF, h))

    blk = lax.broadcasted_iota(jnp.int32, (N_BLOCKS, TILE), 0)
    past = blk < i
    base_pos = MOBA_BLOCK * i.astype(F32)
    for h in range(N_MOBA):
        hh = N_FOX + h
        slope = _alibi_slope(h)
        q_t = head_rows(ROW_QM, h)
        km = kmean_ref[h, :, 0:HEAD_DIM]
        km_hi, km_mid, _ = _split3(km)
        q_hi, q_mid, _ = _split3(q_t)
        g = (_dot(km_hi.astype(BF16), q_hi.astype(BF16))
             + _dot(km_hi.astype(BF16), q_mid.astype(BF16))
             + _dot(km_mid.astype(BF16), q_hi.astype(BF16)))
        remaining = past
        for _ in range(MOBA_TOPK):
            gm = jnp.where(remaining, g, -jnp.inf)
            mx = jnp.max(gm, axis=0, keepdims=True)
            cand = remaining & (gm == mx)
            first = jnp.min(jnp.where(cand, blk, N_BLOCKS), axis=0, keepdims=True)
            remaining = remaining & (blk != first)
        selected = (past & jnp.logical_not(remaining)) | (blk == i)
        r = row64 - MOBA_ALIBI
        d_hi, d_mid, d_lo = _split3((slope * LOG2E) * (base_pos + pos64))
        q_aug = jnp.where(r == 0, -d_hi,
                jnp.where(r == 1, -d_mid,
                jnp.where(r == 2, -d_lo,
                jnp.where((r >= 3) & (r < 6), 1.0, 0.0))))
        q_aug = jnp.concatenate([jnp.where(selected, 0.0, NEG_INF), q_aug[MOBA_ALIBI:]], axis=0)
        k_aug = jnp.where(row64 == MOBA_SEL + i, 1.0,
                jnp.where((r >= 0) & (r < 3), 1.0,
                jnp.where(r == 3, d_hi,
                jnp.where(r == 4, d_mid,
                jnp.where(r == 5, d_lo, 0.0)))))
        k_nat = store_head(hh, q_t, q_aug, head_rows(ROW_KM, h), k_aug, head_rows(ROW_VM, h))
        kmean_ref[h, pl.ds(i, 1), :] = jnp.sum(k_nat, axis=0, keepdims=True) * (1.0 / MOBA_BLOCK)


def _projection(x, wt, bf):
    B, S, _ = x.shape
    n_t = S // TILE
    q_per = Q_TILE // TILE
    const = lambda b, i: (0, 0)
    return pl.pallas_call(
        _proj_kernel,
        grid=(B, n_t),
        in_specs=[
            pl.BlockSpec((1, TILE, D_MODEL), lambda b, i: (b, i, 0)),
            pl.BlockSpec(wt.shape, const, pipeline_mode=pl.Buffered(1)),
            pl.BlockSpec(bf.shape, const, pipeline_mode=pl.Buffered(1)),
        ],
        out_specs=[
            pl.BlockSpec((1, N_HEADS, 1, TILE, K_WIDTH), lambda b, i: (b, 0, i, 0, 0)),
            pl.BlockSpec((1, N_HEADS, 1, K_WIDTH, TILE), lambda b, i: (b, 0, i // q_per, 0, i % q_per)),
            pl.BlockSpec((1, N_HEADS, 1, V_ROWS, TILE), lambda b, i: (b, 0, i, 0, 0)),
        ],
        out_shape=[
            jax.ShapeDtypeStruct((B, N_HEADS, n_t, TILE, K_WIDTH), BF16),
            jax.ShapeDtypeStruct((B, N_HEADS, S // Q_TILE, K_WIDTH, Q_TILE), BF16),
            jax.ShapeDtypeStruct((B, N_HEADS, n_t, V_ROWS, TILE), BF16),
        ],
        scratch_shapes=[
            pltpu.VMEM((LF_ROWS, TILE), F32),
            pltpu.VMEM((N_MOBA, N_BLOCKS, K_WIDTH), F32),
        ],
        compiler_params=pltpu.CompilerParams(
            dimension_semantics=("arbitrary", "arbitrary"), vmem_limit_bytes=VMEM_LIMIT),
        name="projection",
    )(x, wt, bf)


def _attn_kernel(k_ref, qt_ref, vt_ref, o_ref, s_ref, tmax_ref, m_ref, acc_ref):
    qi = pl.program_id(2)
    n_h = qt_ref.shape[1]
    tiles_per_q = Q_TILE // TILE
    first_masked = tiles_per_q * qi
    last_cols = slice(Q_TILE - TILE, Q_TILE)
    m_ref[...] = jnp.full(m_ref.shape, -jnp.inf, F32)
    acc_ref[...] = jnp.zeros(acc_ref.shape, F32)

    def logits(h, kj, slot, masked):
        s = _dot(k_ref[0, h, kj], qt_ref[0, h, 0])
        if masked:
            key_idx = lax.broadcasted_iota(jnp.int32, (TILE, Q_TILE), 0)
            qry_idx = lax.broadcasted_iota(jnp.int32, (TILE, Q_TILE), 1)
            s = jnp.where(key_idx + (kj - first_masked) * TILE <= qry_idx, s, NEG_INF)
        s_ref[slot, h] = s
        tmax_ref[slot, h] = jnp.max(s, axis=0, keepdims=True)

    def accumulate(h, kj, slot):
        m = m_ref[h]
        m_new = jnp.maximum(m, tmax_ref[slot, h])
        p = jnp.exp2(s_ref[slot, h] - m_new).astype(BF16)
        acc_ref[h] = acc_ref[h] * jnp.exp2(m - m_new) + _dot(vt_ref[0, h, kj], p)
        m_ref[h] = m_new

    def logits_last(h, kj, slot):
        s = _dot(k_ref[0, h, kj], qt_ref[0, h, 0, :, last_cols])
        key_idx = lax.broadcasted_iota(jnp.int32, (TILE, TILE), 0)
        qry_idx = lax.broadcasted_iota(jnp.int32, (TILE, TILE), 1)
        s = jnp.where(key_idx <= qry_idx, s, NEG_INF)
        s_ref[slot, h, :, last_cols] = s
        tmax_ref[slot, h, :, last_cols] = jnp.max(s, axis=0, keepdims=True)

    def accumulate_last(h, kj, slot):
        m = m_ref[h, :, last_cols]
        m_new = jnp.maximum(m, tmax_ref[slot, h, :, last_cols])
        p = jnp.exp2(s_ref[slot, h, :, last_cols] - m_new).astype(BF16)
        acc_ref[h, :, last_cols] = acc_ref[h, :, last_cols] * jnp.exp2(m - m_new) + _dot(vt_ref[0, h, kj], p)
        m_ref[h, :, last_cols] = m_new

    def pair(kj, slot, masked):
        for h in range(n_h):
            logits(h, kj + 1, 1 - slot, masked)
            accumulate(h, kj, slot)

    for h in range(n_h):
        logits(h, 0, 0, True)

    @pl.when(qi > 0)
    def _():
        def two_pairs(i, c):
            pair(2 * i, 0, False)
            pair(2 * i + 1, 1, False)
            return c
        lax.fori_loop(0, qi - 1, two_pairs, 0)
        pair(first_masked - 2, 0, False)
        pair(first_masked - 1, 1, True)

    for h in range(n_h):
        logits_last(h, first_masked + 1, 1)
        accumulate(h, first_masked, 0)
    for h in range(n_h):
        accumulate_last(h, first_masked + 1, 1)
    for h in range(n_h):
        acc = acc_ref[h]
        o_ref[0, h, 0] = (acc[0:HEAD_DIM] / acc[HEAD_DIM:HEAD_DIM + 1]).astype(BF16)


def _attention(k, qt, vt):
    B, H, n_t = k.shape[:3]
    n_q = qt.shape[2]
    hps = HEADS_PER_STEP
    resident = lambda r, c, bufs: pl.BlockSpec((1, hps, n_t, r, c), lambda b, g, i: (b, g, 0, 0, 0),
                                               pipeline_mode=pl.Buffered(bufs))
    tiled = lambda r, c: pl.BlockSpec((1, hps, 1, r, c), lambda b, g, i: (b, g, i, 0, 0))
    return pl.pallas_call(
        _attn_kernel,
        grid=(B, H // hps, n_q),
        in_specs=[resident(TILE, K_WIDTH, 1), tiled(K_WIDTH, Q_TILE), resident(V_ROWS, TILE, 2)],
        out_specs=tiled(HEAD_DIM, Q_TILE),
        out_shape=jax.ShapeDtypeStruct((B, H, n_q, HEAD_DIM, Q_TILE), BF16),
        scratch_shapes=[
            pltpu.VMEM((2, hps, TILE, Q_TILE), F32),
            pltpu.VMEM((2, hps, 1, Q_TILE), F32),
            pltpu.VMEM((hps, 1, Q_TILE), F32),
            pltpu.VMEM((hps, V_ROWS, Q_TILE), F32),
        ],
        compiler_params=pltpu.CompilerParams(
            dimension_semantics=("arbitrary", "arbitrary", "arbitrary"), vmem_limit_bytes=VMEM_LIMIT),
        name="attention",
    )(k, qt, vt)


def _layer_norm(z, g, b):
    mu = jnp.mean(z, axis=-1, keepdims=True)
    d = z - mu
    var = jnp.mean(d * d, axis=-1, keepdims=True)
    return d * lax.rsqrt(var + LN_EPS) * g + b


def _outproj_kernel(ot_ref, x_ref, wo_ref, g_ref, b_ref, out_ref):
    halves = []
    for j in range(Q_TILE // TILE):
        heads_t = ot_ref[0, :, 0, :, j * TILE:(j + 1) * TILE].reshape(N_HEADS * HEAD_DIM, TILE)
        halves.append(lax.dot_general(heads_t, wo_ref[...], (((0,), (0,)), ((), ())),
                                      preferred_element_type=F32))
    for j, y in enumerate(halves):
        rows = slice(j * TILE, (j + 1) * TILE)
        out_ref[0, rows] = _layer_norm(ALPHA * x_ref[0, rows] + y, g_ref[...], b_ref[...])


def _outproj_ln(ot, x, wo, g, b):
    B, S, _ = x.shape
    const = lambda bb, i: (0, 0)
    return pl.pallas_call(
        _outproj_kernel,
        grid=(B, S // Q_TILE),
        in_specs=[
            pl.BlockSpec((1, N_HEADS, 1, HEAD_DIM, Q_TILE), lambda bb, i: (bb, 0, i, 0, 0)),
            pl.BlockSpec((1, Q_TILE, D_MODEL), lambda bb, i: (bb, i, 0)),
            pl.BlockSpec(wo.shape, const, pipeline_mode=pl.Buffered(1)),
            pl.BlockSpec(g.shape, const),
            pl.BlockSpec(b.shape, const),
        ],
        out_specs=pl.BlockSpec((1, Q_TILE, D_MODEL), lambda bb, i: (bb, i, 0)),
        out_shape=jax.ShapeDtypeStruct(x.shape, F32),
        compiler_params=pltpu.CompilerParams(
            dimension_semantics=("arbitrary", "arbitrary"), vmem_limit_bytes=VMEM_LIMIT),
        name="outproj_ln",
    )(ot, x, wo, g, b)


def _ffn_kernel(x_ref, wgu_ref, wd_ref, g_ref, b_ref, out_ref):
    x = x_ref[...]
    xb = x.astype(BF16)

    def gate_up(c):
        cols = slice(c * FFN_CHUNK, (c + 1) * FFN_CHUNK)
        cols_up = slice(D_FF + c * FFN_CHUNK, D_FF + (c + 1) * FFN_CHUNK)
        return _dot(xb, wgu_ref[:, cols]), _dot(xb, wgu_ref[:, cols_up])

    n_chunks = D_FF // FFN_CHUNK
    y = jnp.zeros(x.shape, F32)
    nxt = gate_up(0)
    for c in range(n_chunks):
        gate, up = nxt
        if c + 1 < n_chunks:
            nxt = gate_up(c + 1)
        hidden = (gate * jax.nn.sigmoid(gate) * up).astype(BF16)
        y = y + _dot(hidden, wd_ref[c * FFN_CHUNK:(c + 1) * FFN_CHUNK, :])
    out_ref[...] = _layer_norm(ALPHA * x + y, g_ref[...], b_ref[...])


def _ffn_ln(x2d, wgu, wd, g, b):
    n = x2d.shape[0]
    const = lambda i: (0, 0)
    return pl.pallas_call(
        _ffn_kernel,
        grid=(n // FFN_ROWS,),
        in_specs=[
            pl.BlockSpec((FFN_ROWS, D_MODEL), lambda i: (i, 0)),
            pl.BlockSpec(wgu.shape, const, pipeline_mode=pl.Buffered(1)),
            pl.BlockSpec(wd.shape, const, pipeline_mode=pl.Buffered(1)),
            pl.BlockSpec(g.shape, const),
            pl.BlockSpec(b.shape, const),
        ],
        out_specs=pl.BlockSpec((FFN_ROWS, D_MODEL), lambda i: (i, 0)),
        out_shape=jax.ShapeDtypeStruct(x2d.shape, F32),
        compiler_params=pltpu.CompilerParams(
            dimension_semantics=("arbitrary",), vmem_limit_bytes=VMEM_LIMIT),
        name="ffn_ln",
    )(x2d, wgu, wd, g, b)


def _layout_in_proj(w_in_l, b_f_l):
    fq, fk, fv, ff, mq, mk, mv = jnp.split(
        w_in_l, [GROUP_WIDTH, 2 * GROUP_WIDTH, 3 * GROUP_WIDTH, 3 * GROUP_WIDTH + N_FOX,
                 4 * GROUP_WIDTH + N_FOX, 5 * GROUP_WIDTH + N_FOX], axis=1)
    wt = jnp.concatenate([ff.T, jnp.zeros((LF_ROWS - N_FOX, D_MODEL), F32),
                          fq.T, fk.T, fv.T, mq.T, mk.T, mv.T], axis=0).astype(BF16)
    bf = jnp.broadcast_to(jnp.pad(b_f_l.astype(F32), (0, LF_ROWS - N_FOX))[:, None], (LF_ROWS, TILE))
    return wt, bf


def kernel(x, w_in, b_f, w_o, ln1_g, ln1_b, w_gu, w_down, ln2_g, ln2_b):
    B, S, D = x.shape
    for l in range(DEPTH):
        wt, bf = _layout_in_proj(w_in[l], b_f[l])
        k, qt, vt = _projection(x, wt, bf)
        ot = _attention(k, qt, vt)
        x = _outproj_ln(ot, x, w_o[l].astype(BF16), ln1_g[l][None, :], ln1_b[l][None, :])
        x = _ffn_ln(x.reshape(B * S, D), w_gu[l].astype(BF16), w_down[l].astype(BF16),
                    ln2_g[l][None, :], ln2_b[l][None, :]).reshape(B, S, D)
    return x
```

```python
import jax
import jax.numpy as jnp
from jax import lax
from jax.experimental import pallas as pl
from jax.experimental.pallas import tpu as pltpu

D_MODEL = 1024
HEAD_DIM = 64
N_FOX = 8
N_MOBA = 8
N_HEADS = N_FOX + N_MOBA
GROUP_WIDTH = N_FOX * HEAD_DIM
MOBA_BLOCK = 256
MOBA_TOPK = 3
D_FF = 2816
DEPTH = 2
ALPHA = (2 * DEPTH) ** 0.25
LN_EPS = 1e-5
NEG_INF = -1e30
LOG2E = 1.4426950408889634

TILE = 256
Q_TILE = 2 * TILE
K_WIDTH = 128
AUG = K_WIDTH - HEAD_DIM
V_ROWS = 80
LF_ROWS = 16
VMEM_LIMIT = 56 * 1024 * 1024
HEADS_PER_STEP = 8
N_BLOCKS = 32
FFN_ROWS = 512
FFN_CHUNK = 256

F32 = jnp.float32
BF16 = jnp.bfloat16

ROW_LF = 0
ROW_QF, ROW_KF, ROW_VF, ROW_QM, ROW_KM, ROW_VM = (LF_ROWS + j * GROUP_WIDTH for j in range(6))

FOX_Q_C = 0
FOX_K_ONE = 0
MOBA_SEL = 0
MOBA_ALIBI = 32


def _dot(a, b):
    return jnp.dot(a, b, preferred_element_type=F32)


def _dot_nt(a, b):
    return lax.dot_general(a, b, (((1,), (1,)), ((), ())), preferred_element_type=F32)


def _split3(x):
    hi = x.astype(BF16).astype(F32)
    r = x - hi
    mid = r.astype(BF16).astype(F32)
    lo = (r - mid).astype(BF16).astype(F32)
    return hi, mid, lo


def _log_sigmoid(x):
    return jnp.minimum(x, 0.0) - jnp.log1p(jnp.exp(-jnp.abs(x)))


def _alibi_slope(h):
    return 2.0 ** (-8.0 * (h + 1) / N_MOBA)


def _proj_kernel(x_ref, wt_ref, bf_ref, k_ref, qt_ref, vt_ref, carry_ref, kmean_ref):
    i = pl.program_id(1)

    @pl.when(i == 0)
    def _():
        carry_ref[...] = jnp.zeros_like(carry_ref)
        kmean_ref[...] = jnp.zeros_like(kmean_ref)

    xb = x_ref[0].astype(BF16)
    tr = _dot_nt(wt_ref[...], xb)

    def head_rows(base, h):
        return tr[base + h * HEAD_DIM:base + (h + 1) * HEAD_DIM]

    row64 = lax.broadcasted_iota(jnp.int32, (AUG, TILE), 0)
    pos64 = lax.broadcasted_iota(jnp.int32, (AUG, TILE), 1).astype(F32)
    ones_rows = jnp.where(lax.broadcasted_iota(jnp.int32, (V_ROWS - HEAD_DIM, TILE), 0) == 0,
                          1.0, 0.0).astype(BF16)
    scale = HEAD_DIM ** -0.5 * LOG2E

    def store_head(hh, q_t, q_aug, k_t, k_aug, v_t):
        qt_ref[0, hh, 0, 0:HEAD_DIM, :] = (q_t * scale).astype(BF16)
        qt_ref[0, hh, 0, HEAD_DIM:K_WIDTH, :] = q_aug.astype(BF16)
        vt_ref[0, hh, 0, 0:HEAD_DIM, :] = v_t.astype(BF16)
        vt_ref[0, hh, 0, HEAD_DIM:V_ROWS, :] = ones_rows
        k_nat = jnp.concatenate([k_t, k_aug], axis=0).T
        k_ref[0, hh, 0] = k_nat.astype(BF16)
        return k_nat

    lf = _log_sigmoid(tr[ROW_LF:ROW_LF + LF_ROWS] + bf_ref[...])
    tri = (lax.broadcasted_iota(jnp.int32, (TILE, TILE), 0)
           <= lax.broadcasted_iota(jnp.int32, (TILE, TILE), 1)).astype(BF16)
    ones_sq = jnp.ones((TILE, TILE), BF16)
    c_t = carry_ref[...]
    total = jnp.zeros_like(c_t)
    for part in _split3(lf):
        pb = part.astype(BF16)
        c_t = c_t + _dot(pb, tri)
        total = total + _dot(pb, ones_sq)
    carry_ref[...] = carry_ref[...] + total
    c_hi, c_mid, c_lo = _split3(c_t * LOG2E)

    for h in range(N_FOX):
        hi, mid, lo = c_hi[h:h + 1], c_mid[h:h + 1], c_lo[h:h + 1]
        q_aug = jnp.where(row64 == FOX_Q_C, hi,
                jnp.where(row64 == FOX_Q_C + 1, mid,
                jnp.where(row64 == FOX_Q_C + 2, lo,
                jnp.where(row64 < FOX_Q_C + 6, 1.0, 0.0))))
        k_aug = jnp.where(row64 < FOX_K_ONE + 3, 1.0,
                jnp.where(row64 == FOX_K_ONE + 3, -hi,
                jnp.where(row64 == FOX_K_ONE + 4, -mid,
                jnp.where(row64 == FOX_K_ONE + 5, -lo, 0.0))))
        store_head(h, head_rows(ROW_QF, h), q_aug, head_rows(ROW_KF, h), k_aug, head_rows(ROW_VF, h))

    blk = lax.broadcasted_iota(jnp.int32, (N_BLOCKS, TILE), 0)
    past = blk < i
    base_pos = MOBA_BLOCK * i.astype(F32)
    for h in range(N_MOBA):
        hh = N_FOX + h
        slope = _alibi_slope(h)
        q_t = head_rows(ROW_QM, h)
        km = kmean_ref[h, :, 0:HEAD_DIM]
        km_hi, km_mid, _ = _split3(km)
        q_hi, q_mid, _ = _split3(q_t)
        g = (_dot(km_hi.astype(BF16), q_hi.astype(BF16))
             + _dot(km_hi.astype(BF16), q_mid.astype(BF16))
             + _dot(km_mid.astype(BF16), q_hi.astype(BF16)))
        remaining = past
        for _ in range(MOBA_TOPK):
            gm = jnp.where(remaining, g, -jnp.inf)
            mx = jnp.max(gm, axis=0, keepdims=True)
            cand = remaining & (gm == mx)
            first = jnp.min(jnp.where(cand, blk, N_BLOCKS), axis=0, keepdims=True)
            remaining = remaining & (blk != first)
        selected = (past & jnp.logical_not(remaining)) | (blk == i)
        r = row64 - MOBA_ALIBI
        d_hi, d_mid, d_lo = _split3((slope * LOG2E) * (base_pos + pos64))
        q_aug = jnp.where(r == 0, -d_hi,
                jnp.where(r == 1, -d_mid,
                jnp.where(r == 2, -d_lo,
                jnp.where((r >= 3) & (r < 6), 1.0, 0.0))))
        q_aug = jnp.concatenate([jnp.where(selected, 0.0, NEG_INF), q_aug[MOBA_ALIBI:]], axis=0)
        k_aug = jnp.where(row64 == MOBA_SEL + i, 1.0,
                jnp.where((r >= 0) & (r < 3), 1.0,
                jnp.where(r == 3, d_hi,
                jnp.where(r == 4, d_mid,
                jnp.where(r == 5, d_lo, 0.0)))))
        k_nat = store_head(hh, q_t, q_aug, head_rows(ROW_KM, h), k_aug, head_rows(ROW_VM, h))
        kmean_ref[h, pl.ds(i, 1), :] = jnp.sum(k_nat, axis=0, keepdims=True) * (1.0 / MOBA_BLOCK)


def _projection(x, wt, bf):
    B, S, _ = x.shape
    n_t = S // TILE
    q_per = Q_TILE // TILE
    const = lambda b, i: (0, 0)
    return pl.pallas_call(
        _proj_kernel,
        grid=(B, n_t),
        in_specs=[
            pl.BlockSpec((1, TILE, D_MODEL), lambda b, i: (b, i, 0)),
            pl.BlockSpec(wt.shape, const, pipeline_mode=pl.Buffered(1)),
            pl.BlockSpec(bf.shape, const, pipeline_mode=pl.Buffered(1)),
        ],
        out_specs=[
            pl.BlockSpec((1, N_HEADS, 1, TILE, K_WIDTH), lambda b, i: (b, 0, i, 0, 0)),
            pl.BlockSpec((1, N_HEADS, 1, K_WIDTH, TILE), lambda b, i: (b, 0, i // q_per, 0, i % q_per)),
            pl.BlockSpec((1, N_HEADS, 1, V_ROWS, TILE), lambda b, i: (b, 0, i, 0, 0)),
        ],
        out_shape=[
            jax.ShapeDtypeStruct((B, N_HEADS, n_t, TILE, K_WIDTH), BF16),
            jax.ShapeDtypeStruct((B, N_HEADS, S // Q_TILE, K_WIDTH, Q_TILE), BF16),
            jax.ShapeDtypeStruct((B, N_HEADS, n_t, V_ROWS, TILE), BF16),
        ],
        scratch_shapes=[
            pltpu.VMEM((LF_ROWS, TILE), F32),
            pltpu.VMEM((N_MOBA, N_BLOCKS, K_WIDTH), F32),
        ],
        compiler_params=pltpu.CompilerParams(
            dimension_semantics=("arbitrary", "arbitrary"), vmem_limit_bytes=VMEM_LIMIT),
        name="projection",
    )(x, wt, bf)


def _attn_kernel(k_ref, qt_ref, vt_ref, o_ref, s_ref, tmax_ref, m_ref, acc_ref):
    qi = pl.program_id(2)
    n_h = qt_ref.shape[1]
    tiles_per_q = Q_TILE // TILE
    first_masked = tiles_per_q * qi
    last_cols = slice(Q_TILE - TILE, Q_TILE)
    m_ref[...] = jnp.full(m_ref.shape, -jnp.inf, F32)
    acc_ref[...] = jnp.zeros(acc_ref.shape, F32)

    def logits(h, kj, slot, masked):
        s = _dot(k_ref[0, h, kj], qt_ref[0, h, 0])
        if masked:
            key_idx = lax.broadcasted_iota(jnp.int32, (TILE, Q_TILE), 0)
            qry_idx = lax.broadcasted_iota(jnp.int32, (TILE, Q_TILE), 1)
            s = jnp.where(key_idx + (kj - first_masked) * TILE <= qry_idx, s, NEG_INF)
        s_ref[slot, h] = s
        tmax_ref[slot, h] = jnp.max(s, axis=0, keepdims=True)

    def accumulate(h, kj, slot):
        m = m_ref[h]
        m_new = jnp.maximum(m, tmax_ref[slot, h])
        p = jnp.exp2(s_ref[slot, h] - m_new).astype(BF16)
        acc_ref[h] = acc_ref[h] * jnp.exp2(m - m_new) + _dot(vt_ref[0, h, kj], p)
        m_ref[h] = m_new

    def logits_last(h, kj, slot):
        s = _dot(k_ref[0, h, kj], qt_ref[0, h, 0, :, last_cols])
        key_idx = lax.broadcasted_iota(jnp.int32, (TILE, TILE), 0)
        qry_idx = lax.broadcasted_iota(jnp.int32, (TILE, TILE), 1)
        s = jnp.where(key_idx <= qry_idx, s, NEG_INF)
        s_ref[slot, h, :, last_cols] = s
        tmax_ref[slot, h, :, last_cols] = jnp.max(s, axis=0, keepdims=True)

    def accumulate_last(h, kj, slot):
        m = m_ref[h, :, last_cols]
        m_new = jnp.maximum(m, tmax_ref[slot, h, :, last_cols])
        p = jnp.exp2(s_ref[slot, h, :, last_cols] - m_new).astype(BF16)
        acc_ref[h, :, last_cols] = acc_ref[h, :, last_cols] * jnp.exp2(m - m_new) + _dot(vt_ref[0, h, kj], p)
        m_ref[h, :, last_cols] = m_new

    def pair(kj, slot, masked):
        for h in range(n_h):
            logits(h, kj + 1, 1 - slot, masked)
            accumulate(h, kj, slot)

    for h in range(n_h):
        logits(h, 0, 0, True)

    @pl.when(qi > 0)
    def _():
        def four_pairs(i, c):
            for j in range(4):
                pair(4 * i + j, j % 2, False)
            return c
        n_four = (qi - 1) // 2
        lax.fori_loop(0, n_four, four_pairs, 0)

        @pl.when((qi - 1) % 2 == 1)
        def _():
            pair(4 * n_four, 0, False)
            pair(4 * n_four + 1, 1, False)

        pair(first_masked - 2, 0, False)
        pair(first_masked - 1, 1, True)

    for h in range(n_h):
        logits_last(h, first_masked + 1, 1)
        accumulate(h, first_masked, 0)
    for h in range(n_h):
        accumulate_last(h, first_masked + 1, 1)
    for h in range(n_h):
        acc = acc_ref[h]
        o_ref[0, h, 0] = (acc[0:HEAD_DIM] / acc[HEAD_DIM:HEAD_DIM + 1]).astype(BF16)


def _attention(k, qt, vt):
    B, H, n_t = k.shape[:3]
    n_q = qt.shape[2]
    hps = HEADS_PER_STEP
    resident = lambda r, c, bufs: pl.BlockSpec((1, hps, n_t, r, c), lambda b, g, i: (b, g, 0, 0, 0),
                                               pipeline_mode=pl.Buffered(bufs))
    tiled = lambda r, c: pl.BlockSpec((1, hps, 1, r, c), lambda b, g, i: (b, g, i, 0, 0))
    return pl.pallas_call(
        _attn_kernel,
        grid=(B, H // hps, n_q),
        in_specs=[resident(TILE, K_WIDTH, 1), tiled(K_WIDTH, Q_TILE), resident(V_ROWS, TILE, 2)],
        out_specs=tiled(HEAD_DIM, Q_TILE),
        out_shape=jax.ShapeDtypeStruct((B, H, n_q, HEAD_DIM, Q_TILE), BF16),
        scratch_shapes=[
            pltpu.VMEM((2, hps, TILE, Q_TILE), F32),
            pltpu.VMEM((2, hps, 1, Q_TILE), F32),
            pltpu.VMEM((hps, 1, Q_TILE), F32),
            pltpu.VMEM((hps, V_ROWS, Q_TILE), F32),
        ],
        compiler_params=pltpu.CompilerParams(
            dimension_semantics=("arbitrary", "arbitrary", "arbitrary"), vmem_limit_bytes=VMEM_LIMIT),
        name="attention",
    )(k, qt, vt)


def _layer_norm(z, g, b):
    mu = jnp.mean(z, axis=-1, keepdims=True)
    d = z - mu
    var = jnp.mean(d * d, axis=-1, keepdims=True)
    return d * lax.rsqrt(var + LN_EPS) * g + b


def _outproj_kernel(ot_ref, x_ref, wo_ref, g_ref, b_ref, out_ref):
    halves = []
    for j in range(Q_TILE // TILE):
        heads_t = ot_ref[0, :, 0, :, j * TILE:(j + 1) * TILE].reshape(N_HEADS * HEAD_DIM, TILE)
        halves.append(lax.dot_general(heads_t, wo_ref[...], (((0,), (0,)), ((), ())),
                                      preferred_element_type=F32))
    for j, y in enumerate(halves):
        rows = slice(j * TILE, (j + 1) * TILE)
        out_ref[0, rows] = _layer_norm(ALPHA * x_ref[0, rows] + y, g_ref[...], b_ref[...])


def _outproj_ln(ot, x, wo, g, b):
    B, S, _ = x.shape
    const = lambda bb, i: (0, 0)
    return pl.pallas_call(
        _outproj_kernel,
        grid=(B, S // Q_TILE),
        in_specs=[
            pl.BlockSpec((1, N_HEADS, 1, HEAD_DIM, Q_TILE), lambda bb, i: (bb, 0, i, 0, 0)),
            pl.BlockSpec((1, Q_TILE, D_MODEL), lambda bb, i: (bb, i, 0)),
            pl.BlockSpec(wo.shape, const, pipeline_mode=pl.Buffered(1)),
            pl.BlockSpec(g.shape, const),
            pl.BlockSpec(b.shape, const),
        ],
        out_specs=pl.BlockSpec((1, Q_TILE, D_MODEL), lambda bb, i: (bb, i, 0)),
        out_shape=jax.ShapeDtypeStruct(x.shape, F32),
        compiler_params=pltpu.CompilerParams(
            dimension_semantics=("arbitrary", "arbitrary"), vmem_limit_bytes=VMEM_LIMIT),
        name="outproj_ln",
    )(ot, x, wo, g, b)


def _ffn_kernel(x_ref, wgu_ref, wd_ref, g_ref, b_ref, out_ref):
    x = x_ref[...]
    xb = x.astype(BF16)

    def gate_up(c):
        cols = slice(c * FFN_CHUNK, (c + 1) * FFN_CHUNK)
        cols_up = slice(D_FF + c * FFN_CHUNK, D_FF + (c + 1) * FFN_CHUNK)
        return _dot(xb, wgu_ref[:, cols]), _dot(xb, wgu_ref[:, cols_up])

    n_chunks = D_FF // FFN_CHUNK
    y = jnp.zeros(x.shape, F32)
    nxt = gate_up(0)
    for c in range(n_chunks):
        gate, up = nxt
        if c + 1 < n_chunks:
            nxt = gate_up(c + 1)
        hidden = (gate * jax.nn.sigmoid(gate) * up).astype(BF16)
        y = y + _dot(hidden, wd_ref[c * FFN_CHUNK:(c + 1) * FFN_CHUNK, :])
    out_ref[...] = _layer_norm(ALPHA * x + y, g_ref[...], b_ref[...])


def _ffn_ln(x2d, wgu, wd, g, b):
    n = x2d.shape[0]
    const = lambda i: (0, 0)
    return pl.pallas_call(
        _ffn_kernel,
        grid=(n // FFN_ROWS,),
        in_specs=[
            pl.BlockSpec((FFN_ROWS, D_MODEL), lambda i: (i, 0)),
            pl.BlockSpec(wgu.shape, const, pipeline_mode=pl.Buffered(1)),
            pl.BlockSpec(wd.shape, const, pipeline_mode=pl.Buffered(1)),
            pl.BlockSpec(g.shape, const),
            pl.BlockSpec(b.shape, const),
        ],
        out_specs=pl.BlockSpec((FFN_ROWS, D_MODEL), lambda i: (i, 0)),
        out_shape=jax.ShapeDtypeStruct(x2d.shape, F32),
        compiler_params=pltpu.CompilerParams(
            dimension_semantics=("arbitrary",), vmem_limit_bytes=VMEM_LIMIT),
        name="ffn_ln",
    )(x2d, wgu, wd, g, b)


def _layout_in_proj(w_in_l, b_f_l):
    fq, fk, fv, ff, mq, mk, mv = jnp.split(
        w_in_l, [GROUP_WIDTH, 2 * GROUP_WIDTH, 3 * GROUP_WIDTH, 3 * GROUP_WIDTH + N_FOX,
                 4 * GROUP_WIDTH + N_FOX, 5 * GROUP_WIDTH + N_FOX], axis=1)
    wt = jnp.concatenate([ff.T, jnp.zeros((LF_ROWS - N_FOX, D_MODEL), F32),
                          fq.T, fk.T, fv.T, mq.T, mk.T, mv.T], axis=0).astype(BF16)
    bf = jnp.broadcast_to(jnp.pad(b_f_l.astype(F32), (0, LF_ROWS - N_FOX))[:, None], (LF_ROWS, TILE))
    return wt, bf


def kernel(x, w_in, b_f, w_o, ln1_g, ln1_b, w_gu, w_down, ln2_g, ln2_b):
    B, S, D = x.shape
    for l in range(DEPTH):
        wt, bf = _layout_in_proj(w_in[l], b_f[l])
        k, qt, vt = _projection(x, wt, bf)
        ot = _attention(k, qt, vt)
        x = _outproj_ln(ot, x, w_o[l].astype(BF16), ln1_g[l][None, :], ln1_b[l][None, :])
        x = _ffn_ln(x.reshape(B * S, D), w_gu[l].astype(BF16), w_down[l].astype(BF16),
                    ln2_g[l][None, :], ln2_b[l][None, :]).reshape(B, S, D)
    return x
```

```python
import jax
import jax.numpy as jnp
from jax import lax
from jax.experimental import pallas as pl
from jax.experimental.pallas import tpu as pltpu

D_MODEL = 1024
HEAD_DIM = 64
N_FOX = 8
N_MOBA = 8
N_HEADS = N_FOX + N_MOBA
GROUP_WIDTH = N_FOX * HEAD_DIM
MOBA_BLOCK = 256
MOBA_TOPK = 3
D_FF = 2816
DEPTH = 2
ALPHA = (2 * DEPTH) ** 0.25
LN_EPS = 1e-5
NEG_INF = -1e30
LOG2E = 1.4426950408889634

TILE = 256
Q_TILE = 2 * TILE
K_WIDTH = 128
AUG = K_WIDTH - HEAD_DIM
V_ROWS = 80
LF_ROWS = 16
VMEM_LIMIT = 56 * 1024 * 1024
HEADS_PER_STEP = 8
N_BLOCKS = 32
FFN_ROWS = 512
FFN_CHUNK = 256

F32 = jnp.float32
BF16 = jnp.bfloat16

ROW_LF = 0
ROW_QF, ROW_KF, ROW_VF, ROW_QM, ROW_KM, ROW_VM = (LF_ROWS + j * GROUP_WIDTH for j in range(6))

FOX_Q_C = 0
FOX_K_ONE = 0
MOBA_SEL = 0
MOBA_ALIBI = 32


def _dot(a, b):
    return jnp.dot(a, b, preferred_element_type=F32)


def _dot_nt(a, b):
    return lax.dot_general(a, b, (((1,), (1,)), ((), ())), preferred_element_type=F32)


def _split3(x):
    hi = x.astype(BF16).astype(F32)
    r = x - hi
    mid = r.astype(BF16).astype(F32)
    lo = (r - mid).astype(BF16).astype(F32)
    return hi, mid, lo


def _log_sigmoid(x):
    return jnp.minimum(x, 0.0) - jnp.log1p(jnp.exp(-jnp.abs(x)))


def _alibi_slope(h):
    return 2.0 ** (-8.0 * (h + 1) / N_MOBA)


def _proj_kernel(x_ref, wt_ref, bf_ref, k_ref, qt_ref, vt_ref, carry_ref, kmean_ref):
    i = pl.program_id(1)

    @pl.when(i == 0)
    def _():
        carry_ref[...] = jnp.zeros_like(carry_ref)
        kmean_ref[...] = jnp.zeros_like(kmean_ref)

    xb = x_ref[0].astype(BF16)
    tr = _dot_nt(wt_ref[...], xb)

    def head_rows(base, h):
        return tr[base + h * HEAD_DIM:base + (h + 1) * HEAD_DIM]

    row64 = lax.broadcasted_iota(jnp.int32, (AUG, TILE), 0)
    pos64 = lax.broadcasted_iota(jnp.int32, (AUG, TILE), 1).astype(F32)
    ones_rows = jnp.where(lax.broadcasted_iota(jnp.int32, (V_ROWS - HEAD_DIM, TILE), 0) == 0,
                          1.0, 0.0).astype(BF16)
    scale = HEAD_DIM ** -0.5 * LOG2E

    def store_head(hh, q_t, q_aug, k_t, k_aug, v_t):
        qt_ref[0, hh, 0, 0:HEAD_DIM, :] = (q_t * scale).astype(BF16)
        qt_ref[0, hh, 0, HEAD_DIM:K_WIDTH, :] = q_aug.astype(BF16)
        vt_ref[0, hh, 0, 0:HEAD_DIM, :] = v_t.astype(BF16)
        vt_ref[0, hh, 0, HEAD_DIM:V_ROWS, :] = ones_rows
        k_nat = jnp.concatenate([k_t, k_aug], axis=0).T
        k_ref[0, hh, 0] = k_nat.astype(BF16)
        return k_nat

    lf = _log_sigmoid(tr[ROW_LF:ROW_LF + LF_ROWS] + bf_ref[...])
    tri = (lax.broadcasted_iota(jnp.int32, (TILE, TILE), 0)
           <= lax.broadcasted_iota(jnp.int32, (TILE, TILE), 1)).astype(BF16)
    ones_sq = jnp.ones((TILE, TILE), BF16)
    c_t = carry_ref[...]
    total = jnp.zeros_like(c_t)
    for part in _split3(lf):
        pb = part.astype(BF16)
        c_t = c_t + _dot(pb, tri)
        total = total + _dot(pb, ones_sq)
    carry_ref[...] = carry_ref[...] + total
    c_hi, c_mid, c_lo = _split3(c_t * LOG2E)

    for h in range(N_FOX):
        hi, mid, lo = c_hi[h:h + 1], c_mid[h:h + 1], c_lo[h:h + 1]
        q_aug = jnp.where(row64 == FOX_Q_C, hi,
                jnp.where(row64 == FOX_Q_C + 1, mid,
                jnp.where(row64 == FOX_Q_C + 2, lo,
                jnp.where(row64 < FOX_Q_C + 6, 1.0, 0.0))))
        k_aug = jnp.where(row64 < FOX_K_ONE + 3, 1.0,
                jnp.where(row64 == FOX_K_ONE + 3, -hi,
                jnp.where(row64 == FOX_K_ONE + 4, -mid,
                jnp.where(row64 == FOX_K_ONE + 5, -lo, 0.0))))
        store_head(h, head_rows(ROW_QF, h), q_aug, head_rows(ROW_KF, h), k_aug, head_rows(ROW_VF, h))

    blk = lax.broadcasted_iota(jnp.int32, (N_BLOCKS, TILE), 0)
    past = blk < i
    base_pos = MOBA_BLOCK * i.astype(F32)
    for h in range(N_MOBA):
        hh = N_FOX + h
        slope = _alibi_slope(h)
        q_t = head_rows(ROW_QM, h)
        km = kmean_ref[h, :, 0:HEAD_DIM]
        km_hi, km_mid, _ = _split3(km)
        q_hi, q_mid, _ = _split3(q_t)
        g = (_dot(km_hi.astype(BF16), q_hi.astype(BF16))
             + _dot(km_hi.astype(BF16), q_mid.astype(BF16))
             + _dot(km_mid.astype(BF16), q_hi.astype(BF16)))
        remaining = past
        for _ in range(MOBA_TOPK):
            gm = jnp.where(remaining, g, -jnp.inf)
            mx = jnp.max(gm, axis=0, keepdims=True)
            cand = remaining & (gm == mx)
            first = jnp.min(jnp.where(cand, blk, N_BLOCKS), axis=0, keepdims=True)
            remaining = remaining & (blk != first)
        selected = (past & jnp.logical_not(remaining)) | (blk == i)
        r = row64 - MOBA_ALIBI
        d_hi, d_mid, d_lo = _split3((slope * LOG2E) * (base_pos + pos64))
        q_aug = jnp.where(r == 0, -d_hi,
                jnp.where(r == 1, -d_mid,
                jnp.where(r == 2, -d_lo,
                jnp.where((r >= 3) & (r < 6), 1.0, 0.0))))
        q_aug = jnp.concatenate([jnp.where(selected, 0.0, NEG_INF), q_aug[MOBA_ALIBI:]], axis=0)
        k_aug = jnp.where(row64 == MOBA_SEL + i, 1.0,
                jnp.where((r >= 0) & (r < 3), 1.0,
                jnp.where(r == 3, d_hi,
                jnp.where(r == 4, d_mid,
                jnp.where(r == 5, d_lo, 0.0)))))
        k_nat = store_head(hh, q_t, q_aug, head_rows(ROW_KM, h), k_aug, head_rows(ROW_VM, h))
        kmean_ref[h, pl.ds(i, 1), :] = jnp.sum(k_nat, axis=0, keepdims=True) * (1.0 / MOBA_BLOCK)


def _projection(x, wt, bf):
    B, S, _ = x.shape
    n_t = S // TILE
    q_per = Q_TILE // TILE
    const = lambda b, i: (0, 0)
    return pl.pallas_call(
        _proj_kernel,
        grid=(B, n_t),
        in_specs=[
            pl.BlockSpec((1, TILE, D_MODEL), lambda b, i: (b, i, 0)),
            pl.BlockSpec(wt.shape, const, pipeline_mode=pl.Buffered(1)),
            pl.BlockSpec(bf.shape, const, pipeline_mode=pl.Buffered(1)),
        ],
        out_specs=[
            pl.BlockSpec((1, N_HEADS, 1, TILE, K_WIDTH), lambda b, i: (b, 0, i, 0, 0)),
            pl.BlockSpec((1, N_HEADS, 1, K_WIDTH, TILE), lambda b, i: (b, 0, i // q_per, 0, i % q_per)),
            pl.BlockSpec((1, N_HEADS, 1, V_ROWS, TILE), lambda b, i: (b, 0, i, 0, 0)),
        ],
        out_shape=[
            jax.ShapeDtypeStruct((B, N_HEADS, n_t, TILE, K_WIDTH), BF16),
            jax.ShapeDtypeStruct((B, N_HEADS, S // Q_TILE, K_WIDTH, Q_TILE), BF16),
            jax.ShapeDtypeStruct((B, N_HEADS, n_t, V_ROWS, TILE), BF16),
        ],
        scratch_shapes=[
            pltpu.VMEM((LF_ROWS, TILE), F32),
            pltpu.VMEM((N_MOBA, N_BLOCKS, K_WIDTH), F32),
        ],
        compiler_params=pltpu.CompilerParams(
            dimension_semantics=("arbitrary", "arbitrary"), vmem_limit_bytes=VMEM_LIMIT),
        name="projection",
    )(x, wt, bf)


def _attn_kernel(k_ref, qt_ref, vt_ref, o_ref, s_ref, tmax_ref, m_ref, acc_ref):
    qi = pl.program_id(2)
    n_h = qt_ref.shape[1]
    tiles_per_q = Q_TILE // TILE
    first_masked = tiles_per_q * qi
    last_cols = slice(Q_TILE - TILE, Q_TILE)
    m_ref[...] = jnp.full(m_ref.shape, -jnp.inf, F32)
    acc_ref[...] = jnp.zeros(acc_ref.shape, F32)

    def logits(h, kj, slot, masked):
        s = _dot(k_ref[0, h, kj], qt_ref[0, h, 0])
        if masked:
            key_idx = lax.broadcasted_iota(jnp.int32, (TILE, Q_TILE), 0)
            qry_idx = lax.broadcasted_iota(jnp.int32, (TILE, Q_TILE), 1)
            s = jnp.where(key_idx + (kj - first_masked) * TILE <= qry_idx, s, NEG_INF)
        s_ref[slot, h] = s
        tmax_ref[slot, h] = jnp.max(s, axis=0, keepdims=True)

    def accumulate(h, kj, slot):
        m = m_ref[h]
        m_new = jnp.maximum(m, tmax_ref[slot, h])
        p = jnp.exp2(s_ref[slot, h] - m_new).astype(BF16)
        acc_ref[h] = acc_ref[h] * jnp.exp2(m - m_new) + _dot(vt_ref[0, h, kj], p)
        m_ref[h] = m_new

    def logits_last(h, kj, slot):
        s = _dot(k_ref[0, h, kj], qt_ref[0, h, 0, :, last_cols])
        key_idx = lax.broadcasted_iota(jnp.int32, (TILE, TILE), 0)
        qry_idx = lax.broadcasted_iota(jnp.int32, (TILE, TILE), 1)
        s = jnp.where(key_idx <= qry_idx, s, NEG_INF)
        s_ref[slot, h, :, last_cols] = s
        tmax_ref[slot, h, :, last_cols] = jnp.max(s, axis=0, keepdims=True)

    def accumulate_last(h, kj, slot):
        m = m_ref[h, :, last_cols]
        m_new = jnp.maximum(m, tmax_ref[slot, h, :, last_cols])
        p = jnp.exp2(s_ref[slot, h, :, last_cols] - m_new).astype(BF16)
        acc_ref[h, :, last_cols] = acc_ref[h, :, last_cols] * jnp.exp2(m - m_new) + _dot(vt_ref[0, h, kj], p)
        m_ref[h, :, last_cols] = m_new

    def pair(kj, slot, masked):
        for h in range(n_h):
            logits(h, kj + 1, 1 - slot, masked)
            accumulate(h, kj, slot)

    for h in range(n_h):
        logits(h, 0, 0, True)

    @pl.when(qi > 0)
    def _():
        def pairs_from(first, count):
            for j in range(count):
                pair(first + j, j % 2, False)

        def eight_pairs(i, c):
            pairs_from(8 * i, 8)
            return c
        n_eight = (qi - 1) // 4
        lax.fori_loop(0, n_eight, eight_pairs, 0)
        rest = (qi - 1) % 4

        @pl.when(rest >= 2)
        def _():
            pairs_from(8 * n_eight, 4)

        @pl.when(rest % 2 == 1)
        def _():
            pairs_from(8 * n_eight + 4 * (rest // 2), 2)

        pair(first_masked - 2, 0, False)
        pair(first_masked - 1, 1, True)

    for h in range(n_h):
        logits_last(h, first_masked + 1, 1)
        accumulate(h, first_masked, 0)
    for h in range(n_h):
        accumulate_last(h, first_masked + 1, 1)
    for h in range(n_h):
        acc = acc_ref[h]
        o_ref[0, h, 0] = (acc[0:HEAD_DIM] / acc[HEAD_DIM:HEAD_DIM + 1]).astype(BF16)


def _attention(k, qt, vt):
    B, H, n_t = k.shape[:3]
    n_q = qt.shape[2]
    hps = HEADS_PER_STEP
    resident = lambda r, c, bufs: pl.BlockSpec((1, hps, n_t, r, c), lambda b, g, i: (b, g, 0, 0, 0),
                                               pipeline_mode=pl.Buffered(bufs))
    tiled = lambda r, c: pl.BlockSpec((1, hps, 1, r, c), lambda b, g, i: (b, g, i, 0, 0))
    return pl.pallas_call(
        _attn_kernel,
        grid=(B, H // hps, n_q),
        in_specs=[resident(TILE, K_WIDTH, 1), tiled(K_WIDTH, Q_TILE), resident(V_ROWS, TILE, 2)],
        out_specs=tiled(HEAD_DIM, Q_TILE),
        out_shape=jax.ShapeDtypeStruct((B, H, n_q, HEAD_DIM, Q_TILE), BF16),
        scratch_shapes=[
            pltpu.VMEM((2, hps, TILE, Q_TILE), F32),
            pltpu.VMEM((2, hps, 1, Q_TILE), F32),
            pltpu.VMEM((hps, 1, Q_TILE), F32),
            pltpu.VMEM((hps, V_ROWS, Q_TILE), F32),
        ],
        compiler_params=pltpu.CompilerParams(
            dimension_semantics=("arbitrary", "arbitrary", "arbitrary"), vmem_limit_bytes=VMEM_LIMIT),
        name="attention",
    )(k, qt, vt)


def _layer_norm(z, g, b):
    mu = jnp.mean(z, axis=-1, keepdims=True)
    d = z - mu
    var = jnp.mean(d * d, axis=-1, keepdims=True)
    return d * lax.rsqrt(var + LN_EPS) * g + b


def _outproj_kernel(ot_ref, x_ref, wo_ref, g_ref, b_ref, out_ref):
    halves = []
    for j in range(Q_TILE // TILE):
        heads_t = ot_ref[0, :, 0, :, j * TILE:(j + 1) * TILE].reshape(N_HEADS * HEAD_DIM, TILE)
        halves.append(lax.dot_general(heads_t, wo_ref[...], (((0,), (0,)), ((), ())),
                                      preferred_element_type=F32))
    for j, y in enumerate(halves):
        rows = slice(j * TILE, (j + 1) * TILE)
        out_ref[0, rows] = _layer_norm(ALPHA * x_ref[0, rows] + y, g_ref[...], b_ref[...])


def _outproj_ln(ot, x, wo, g, b):
    B, S, _ = x.shape
    const = lambda bb, i: (0, 0)
    return pl.pallas_call(
        _outproj_kernel,
        grid=(B, S // Q_TILE),
        in_specs=[
            pl.BlockSpec((1, N_HEADS, 1, HEAD_DIM, Q_TILE), lambda bb, i: (bb, 0, i, 0, 0)),
            pl.BlockSpec((1, Q_TILE, D_MODEL), lambda bb, i: (bb, i, 0)),
            pl.BlockSpec(wo.shape, const, pipeline_mode=pl.Buffered(1)),
            pl.BlockSpec(g.shape, const),
            pl.BlockSpec(b.shape, const),
        ],
        out_specs=pl.BlockSpec((1, Q_TILE, D_MODEL), lambda bb, i: (bb, i, 0)),
        out_shape=jax.ShapeDtypeStruct(x.shape, F32),
        compiler_params=pltpu.CompilerParams(
            dimension_semantics=("arbitrary", "arbitrary"), vmem_limit_bytes=VMEM_LIMIT),
        name="outproj_ln",
    )(ot, x, wo, g, b)


def _ffn_kernel(x_ref, wgu_ref, wd_ref, g_ref, b_ref, out_ref):
    x = x_ref[...]
    xb = x.astype(BF16)

    def gate_up(c):
        cols = slice(c * FFN_CHUNK, (c + 1) * FFN_CHUNK)
        cols_up = slice(D_FF + c * FFN_CHUNK, D_FF + (c + 1) * FFN_CHUNK)
        return _dot(xb, wgu_ref[:, cols]), _dot(xb, wgu_ref[:, cols_up])

    n_chunks = D_FF // FFN_CHUNK
    y = jnp.zeros(x.shape, F32)
    nxt = gate_up(0)
    for c in range(n_chunks):
        gate, up = nxt
        if c + 1 < n_chunks:
            nxt = gate_up(c + 1)
        hidden = (gate * jax.nn.sigmoid(gate) * up).astype(BF16)
        y = y + _dot(hidden, wd_ref[c * FFN_CHUNK:(c + 1) * FFN_CHUNK, :])
    out_ref[...] = _layer_norm(ALPHA * x + y, g_ref[...], b_ref[...])


def _ffn_ln(x2d, wgu, wd, g, b):
    n = x2d.shape[0]
    const = lambda i: (0, 0)
    return pl.pallas_call(
        _ffn_kernel,
        grid=(n // FFN_ROWS,),
        in_specs=[
            pl.BlockSpec((FFN_ROWS, D_MODEL), lambda i: (i, 0)),
            pl.BlockSpec(wgu.shape, const, pipeline_mode=pl.Buffered(1)),
            pl.BlockSpec(wd.shape, const, pipeline_mode=pl.Buffered(1)),
            pl.BlockSpec(g.shape, const),
            pl.BlockSpec(b.shape, const),
        ],
        out_specs=pl.BlockSpec((FFN_ROWS, D_MODEL), lambda i: (i, 0)),
        out_shape=jax.ShapeDtypeStruct(x2d.shape, F32),
        compiler_params=pltpu.CompilerParams(
            dimension_semantics=("arbitrary",), vmem_limit_bytes=VMEM_LIMIT),
        name="ffn_ln",
    )(x2d, wgu, wd, g, b)


def _layout_in_proj(w_in_l, b_f_l):
    fq, fk, fv, ff, mq, mk, mv = jnp.split(
        w_in_l, [GROUP_WIDTH, 2 * GROUP_WIDTH, 3 * GROUP_WIDTH, 3 * GROUP_WIDTH + N_FOX,
                 4 * GROUP_WIDTH + N_FOX, 5 * GROUP_WIDTH + N_FOX], axis=1)
    wt = jnp.concatenate([ff.T, jnp.zeros((LF_ROWS - N_FOX, D_MODEL), F32),
                          fq.T, fk.T, fv.T, mq.T, mk.T, mv.T], axis=0).astype(BF16)
    bf = jnp.broadcast_to(jnp.pad(b_f_l.astype(F32), (0, LF_ROWS - N_FOX))[:, None], (LF_ROWS, TILE))
    return wt, bf


def kernel(x, w_in, b_f, w_o, ln1_g, ln1_b, w_gu, w_down, ln2_g, ln2_b):
    B, S, D = x.shape
    for l in range(DEPTH):
        wt, bf = _layout_in_proj(w_in[l], b_f[l])
        k, qt, vt = _projection(x, wt, bf)
        ot = _attention(k, qt, vt)
        x = _outproj_ln(ot, x, w_o[l].astype(BF16), ln1_g[l][None, :], ln1_b[l][None, :])
        x = _ffn_ln(x.reshape(B * S, D), w_gu[l].astype(BF16), w_down[l].astype(BF16),
                    ln2_g[l][None, :], ln2_b[l][None, :]).reshape(B, S, D)
    return x
```
